```python
import jax, jax.numpy as jnp
from jax import lax
import numpy as np

D_MODEL = 1024
BATCH = 8
SEQ = 2048
DEPTH = 1
DEC_BATCH = 32
DEC_SEQ = 8
PAST_LEN = 16384
PAGE_SIZE = 128

N_HEADS = 8
HEAD_DIM = 64
D_ATTN = N_HEADS * HEAD_DIM
D_CONV = 512
CONV_W = 31
N_IDX_HEADS = 8
IDX_DIM = 32
TOPK_MAX = 256
D_FF = 2816
ROPE_THETA = 10000.0
EPS = 1e-6
Q_BLOCK = 128
IDX_W_SCALE = (N_IDX_HEADS * IDX_DIM) ** -0.5
SPLIT_SIZES = [D_CONV, D_CONV, D_ATTN, D_ATTN, D_ATTN, N_IDX_HEADS * IDX_DIM, IDX_DIM, N_IDX_HEADS, D_MODEL, D_MODEL]
P_IN = sum(SPLIT_SIZES)
SPLIT_AT = [int(i) for i in np.cumsum(SPLIT_SIZES)[:-1]]

kernel_name = "macaron_conv_dsa_hybrid_step"


def rms_norm(x, g):
    xf = x.astype(jnp.float32)
    y = xf * lax.rsqrt(jnp.mean(xf * xf, axis=-1, keepdims=True) + EPS)
    return (y * g.astype(jnp.float32)).astype(x.dtype)


def layer_norm(x, g, b):
    xf = x.astype(jnp.float32)
    mu = jnp.mean(xf, axis=-1, keepdims=True)
    xc = xf - mu
    y = xc * lax.rsqrt(jnp.mean(xc * xc, axis=-1, keepdims=True) + EPS)
    return (y * g.astype(jnp.float32) + b.astype(jnp.float32)).astype(x.dtype)


def rope(x, pos):
    d = x.shape[-1]
    half = d // 2
    inv = jnp.float32(ROPE_THETA) ** (-jnp.arange(half, dtype=jnp.float32) * (2.0 / d))
    ang = pos.astype(jnp.float32)[:, None] * inv[None, :]
    cos = jnp.cos(ang)[:, None, :]
    sin = jnp.sin(ang)[:, None, :]
    xf = x.astype(jnp.float32)
    x1, x2 = xf[..., :half], xf[..., half:]
    return jnp.concatenate([x1 * cos - x2 * sin, x1 * sin + x2 * cos], axis=-1).astype(x.dtype)


def swiglu_ffn(h, w_in, w_out):
    g, u = jnp.split(h @ w_in, 2, axis=-1)
    return (jax.nn.silu(g) * u) @ w_out


def project_in(h, w_in, g_q, g_k, pos):
    B, T, _ = h.shape
    a_glu, b_glu, q, k, v, qi, ki, wi, gc, ga = jnp.split(h @ w_in, SPLIT_AT, axis=-1)
    u = a_glu * jax.nn.sigmoid(b_glu)
    q = rope(rms_norm(q.reshape(B, T, N_HEADS, HEAD_DIM), g_q), pos)
    k = rope(rms_norm(k.reshape(B, T, N_HEADS, HEAD_DIM), g_k), pos)
    v = v.reshape(B, T, N_HEADS, HEAD_DIM)
    qi = rope(qi.reshape(B, T, N_IDX_HEADS, IDX_DIM), pos)
    ki = rope(ki[:, :, None, :], pos)[:, :, 0, :]
    wi = wi * IDX_W_SCALE
    return u, q, k, v, qi, ki, wi, jax.nn.sigmoid(gc), jax.nn.sigmoid(ga)


def conv_branch(u, prev, w_dw, b_dw, g_ln, b_ln, w_out):
    xp = jnp.concatenate([prev, u], axis=1)
    y = lax.conv_general_dilated(xp, w_dw[:, None, :], (1,), 'VALID',
                                 dimension_numbers=('NWC', 'WIO', 'NWC'),
                                 feature_group_count=D_CONV) + b_dw
    y = jax.nn.silu(layer_norm(y, g_ln, b_ln))
    return y @ w_out, xp[:, -(CONV_W - 1):]


def index_scores(qi, wi, ki):
    dots = jnp.einsum('thd,sd->ths', qi, ki)
    return jnp.einsum('th,ths->ts', wi, jax.nn.relu(dots)).astype(jnp.float32)


def select_keys(scores, qpos, n_keys):
    topk = min(TOPK_MAX, n_keys // 4)
    kpos = jnp.arange(n_keys)
    admissible = kpos[None, :] <= qpos[:, None]
    _, idx = lax.top_k(jnp.where(admissible, scores, -jnp.inf), topk)
    return idx, idx <= qpos[:, None]


def gathered_attention(q, k_sel, v_sel, valid):
    logits = jnp.einsum('thd,tkhd->thk', q, k_sel).astype(jnp.float32) * (HEAD_DIM ** -0.5)
    logits = jnp.where(valid[:, None, :], logits, -jnp.inf)
    p = jax.nn.softmax(logits, axis=-1).astype(v_sel.dtype)
    return jnp.einsum('thk,tkhd->thd', p, v_sel)


def prompt_sparse_attention(q, k, v, qi, ki, wi):
    B, T = q.shape[0], q.shape[1]
    nb = T // Q_BLOCK
    qpos_blocks = jnp.arange(T).reshape(nb, Q_BLOCK)

    def one_sequence(args):
        qs, ks, vs, qis, kis, wis = args

        def one_block(blk):
            qb, qib, wib, qpos = blk
            idx, valid = select_keys(index_scores(qib, wib, kis), qpos, T)
            return gathered_attention(qb, ks[idx], vs[idx], valid)

        out = lax.map(one_block, (qs.reshape(nb, Q_BLOCK, N_HEADS, HEAD_DIM),
                                  qis.reshape(nb, Q_BLOCK, N_IDX_HEADS, IDX_DIM),
                                  wis.reshape(nb, Q_BLOCK, N_IDX_HEADS), qpos_blocks))
        return out.reshape(T, N_HEADS, HEAD_DIM)

    return lax.map(one_sequence, (q, k, v, qi, ki, wi))


def sample_sparse_attention(q, k, v, qi, ki, wi, page_table, pool_k, pool_v, pool_ik):
    T = q.shape[1]
    n_past = page_table.shape[1] * PAGE_SIZE
    qpos = n_past + jnp.arange(T)

    def one_sequence(qs, ks, vs, qis, kis, wis, pt):
        ki_all = jnp.concatenate([pool_ik[pt].reshape(n_past, IDX_DIM), kis], axis=0)
        idx, valid = select_keys(index_scores(qis, wis, ki_all), qpos, n_past + T)
        in_past = (idx < n_past)[..., None, None]
        pidx = jnp.minimum(idx, n_past - 1)
        phys = pt[pidx // PAGE_SIZE]
        off = pidx % PAGE_SIZE
        nidx = jnp.clip(idx - n_past, 0, T - 1)
        k_sel = jnp.where(in_past, pool_k[phys, off], ks[nidx])
        v_sel = jnp.where(in_past, pool_v[phys, off], vs[nidx])
        return gathered_attention(qs, k_sel, v_sel, valid)

    return jax.vmap(one_sequence)(q, k, v, qi, ki, wi, page_table)


def merge_branches(x, conv_out, att, gate_c, gate_a, w_aout, w_o):
    B, T = att.shape[0], att.shape[1]
    a = att.reshape(B, T, D_ATTN) @ w_aout
    return x + (gate_c * conv_out + gate_a * a) @ w_o


def setup_inputs(seed: int = 0) -> dict:
    key = jax.random.key(seed)
    ks = jax.random.split(key, 32)
    n_pages = PAST_LEN // PAGE_SIZE
    n_phys = (DEC_BATCH * n_pages * 5) // 4

    def nrm(k, shape, scale):
        return jax.random.normal(k, shape, jnp.float32) * scale

    def gain(k, shape):
        return 1.0 + 0.05 * jax.random.normal(k, shape, jnp.float32)

    perm = jax.random.permutation(ks[6], n_phys)[:DEC_BATCH * n_pages]
    return {
        "x_prompt": nrm(ks[0], (BATCH, SEQ, D_MODEL), 1.0),
        "x_sample": nrm(ks[1], (DEC_BATCH, DEC_SEQ, D_MODEL), 1.0),
        "cache_k": nrm(ks[2], (DEPTH, n_phys, PAGE_SIZE, N_HEADS, HEAD_DIM), 1.0),
        "cache_v": nrm(ks[3], (DEPTH, n_phys, PAGE_SIZE, N_HEADS, HEAD_DIM), 1.0),
        "cache_idx_k": nrm(ks[4], (DEPTH, n_phys, PAGE_SIZE, IDX_DIM), 1.0),
        "state_conv": nrm(ks[5], (DEPTH, DEC_BATCH, CONV_W - 1, D_CONV), 0.5),
        "page_table": perm.reshape(DEC_BATCH, n_pages).astype(jnp.int32),
        "g_ffn1": gain(ks[7], (DEPTH, D_MODEL)),
        "w_ffn1_in": nrm(ks[8], (DEPTH, D_MODEL, 2 * D_FF), D_MODEL ** -0.5),
        "w_ffn1_out": nrm(ks[9], (DEPTH, D_FF, D_MODEL), D_FF ** -0.5),
        "g_mix": gain(ks[10], (DEPTH, D_MODEL)),
        "w_in": nrm(ks[11], (DEPTH, D_MODEL, P_IN), D_MODEL ** -0.5),
        "g_q": gain(ks[12], (DEPTH, HEAD_DIM)),
        "g_k": gain(ks[13], (DEPTH, HEAD_DIM)),
        "w_dw": nrm(ks[14], (DEPTH, CONV_W, D_CONV), CONV_W ** -0.5),
        "b_dw": nrm(ks[15], (DEPTH, D_CONV), 0.02),
        "g_cln": gain(ks[16], (DEPTH, D_CONV)),
        "b_cln": nrm(ks[17], (DEPTH, D_CONV), 0.02),
        "w_cout": nrm(ks[18], (DEPTH, D_CONV, D_MODEL), D_CONV ** -0.5),
        "w_aout": nrm(ks[19], (DEPTH, D_ATTN, D_MODEL), D_ATTN ** -0.5),
        "w_o": nrm(ks[20], (DEPTH, D_MODEL, D_MODEL), D_MODEL ** -0.5),
        "g_ffn2": gain(ks[21], (DEPTH, D_MODEL)),
        "w_ffn2_in": nrm(ks[22], (DEPTH, D_MODEL, 2 * D_FF), D_MODEL ** -0.5),
        "w_ffn2_out": nrm(ks[23], (DEPTH, D_FF, D_MODEL), D_FF ** -0.5),
    }


def reference(x_prompt, x_sample, cache_k, cache_v, cache_idx_k, state_conv, page_table,
              g_ffn1, w_ffn1_in, w_ffn1_out, g_mix, w_in, g_q, g_k, w_dw, b_dw, g_cln, b_cln,
              w_cout, w_aout, w_o, g_ffn2, w_ffn2_in, w_ffn2_out):
    xp, xs = x_prompt, x_sample
    pos_p = jnp.arange(xp.shape[1])
    pos_s = PAST_LEN + jnp.arange(xs.shape[1])
    kp_l, vp_l, ikp_l, cp_l, ks_l, vs_l, iks_l, cs_l = [], [], [], [], [], [], [], []
    for l in range(DEPTH):
        xp = xp + 0.5 * swiglu_ffn(rms_norm(xp, g_ffn1[l]), w_ffn1_in[l], w_ffn1_out[l])
        xs = xs + 0.5 * swiglu_ffn(rms_norm(xs, g_ffn1[l]), w_ffn1_in[l], w_ffn1_out[l])
        up, qp, kp, vp, qip, kip, wip, gcp, gap = project_in(rms_norm(xp, g_mix[l]), w_in[l], g_q[l], g_k[l], pos_p)
        us, qs, kss, vss, qis, kis, wis, gcs, gas = project_in(rms_norm(xs, g_mix[l]), w_in[l], g_q[l], g_k[l], pos_s)
        conv_p, st_p = conv_branch(up, jnp.zeros((up.shape[0], CONV_W - 1, D_CONV), up.dtype),
                                   w_dw[l], b_dw[l], g_cln[l], b_cln[l], w_cout[l])
        conv_s, st_s = conv_branch(us, state_conv[l], w_dw[l], b_dw[l], g_cln[l], b_cln[l], w_cout[l])
        att_p = prompt_sparse_attention(qp, kp, vp, qip, kip, wip)
        att_s = sample_sparse_attention(qs, kss, vss, qis, kis, wis, page_table,
                                        cache_k[l], cache_v[l], cache_idx_k[l])
        xp = merge_branches(xp, conv_p, att_p, gcp, gap, w_aout[l], w_o[l])
        xs = merge_branches(xs, conv_s, att_s, gcs, gas, w_aout[l], w_o[l])
        xp = xp + 0.5 * swiglu_ffn(rms_norm(xp, g_ffn2[l]), w_ffn2_in[l], w_ffn2_out[l])
        xs = xs + 0.5 * swiglu_ffn(rms_norm(xs, g_ffn2[l]), w_ffn2_in[l], w_ffn2_out[l])
        kp_l.append(kp); vp_l.append(vp); ikp_l.append(kip); cp_l.append(st_p)
        ks_l.append(kss); vs_l.append(vss); iks_l.append(kis); cs_l.append(st_s)
    return (xp, xs, jnp.stack(kp_l), jnp.stack(vp_l), jnp.stack(ikp_l), jnp.stack(cp_l),
            jnp.stack(ks_l), jnp.stack(vs_l), jnp.stack(iks_l), jnp.stack(cs_l))
```

```python
import functools
import math

import jax
import jax.numpy as jnp
from jax import lax
from jax.experimental import pallas as pl
from jax.experimental.pallas import tpu as pltpu

N_HEADS = 8
HEAD_DIM = 64
D_ATTN = N_HEADS * HEAD_DIM
D_CONV = 512
CONV_W = 31
N_IDX_HEADS = 8
IDX_DIM = 32
D_IDX = N_IDX_HEADS * IDX_DIM
TOPK_MAX = 256
PAGE_SIZE = 128
ROPE_THETA = 10000.0
EPS = 1e-6
IDX_W_SCALE = D_IDX ** -0.5

LANES = 128
SUBLANES = 8
HALO = 32
NEG_INF_KEY = -2 ** 31 + 0x7FFFFF
MASKED_LOGIT = -1e30
SOFTMAX_FLOOR = 0.25 * MASKED_LOGIT
VMEM_LIMIT = 56 * 1024 * 1024

F32 = jnp.float32
BF16 = jnp.bfloat16
I32 = jnp.int32
NT_DIMS = (((1,), (1,)), ((), ()))


def _row_tile(n, cap):
    t = cap
    while n % t:
        t //= 2
    return t


def _rms(x, g):
    return x * lax.rsqrt(jnp.mean(x * x, axis=-1, keepdims=True) + EPS) * g


def _const_spec(shape):
    nd = len(shape)
    return pl.BlockSpec(shape, lambda *_: (0,) * nd, pipeline_mode=pl.Buffered(1))


def _ffn_body(x_ref, g_ref, win_ref, wout_ref, o_ref, *, d_ff, n_chunks):
    x = x_ref[...]
    h = _rms(x, g_ref[...]).astype(BF16)
    fc = d_ff // n_chunks
    acc = jnp.zeros_like(x)
    for c in range(n_chunks):
        gate = jnp.dot(h, win_ref[:, c * fc:(c + 1) * fc], preferred_element_type=F32)
        up = jnp.dot(h, win_ref[:, d_ff + c * fc:d_ff + (c + 1) * fc], preferred_element_type=F32)
        act = (gate * jax.nn.sigmoid(gate) * up).astype(BF16)
        acc = acc + jnp.dot(act, wout_ref[c * fc:(c + 1) * fc, :], preferred_element_type=F32)
    o_ref[...] = x + 0.5 * acc


def _ffn(x, g, w_in_bf, w_out_bf):
    n, d = x.shape
    d_ff = w_out_bf.shape[0]
    tm = _row_tile(n, 512)
    n_chunks = 2 if d_ff % (2 * LANES) == 0 else 1
    return pl.pallas_call(
        functools.partial(_ffn_body, d_ff=d_ff, n_chunks=n_chunks),
        grid=(n // tm,),
        in_specs=[pl.BlockSpec((tm, d), lambda i: (i, 0)),
                  _const_spec((1, d)), _const_spec((d, 2 * d_ff)), _const_spec((d_ff, d))],
        out_specs=pl.BlockSpec((tm, d), lambda i: (i, 0)),
        out_shape=jax.ShapeDtypeStruct((n, d), F32),
        compiler_params=pltpu.CompilerParams(dimension_semantics=("parallel",), vmem_limit_bytes=VMEM_LIMIT),
        name="ffn",
    )(x, g, w_in_bf, w_out_bf)


def _rope(x, cos, sin_signed, half):
    n = x.shape[-1]
    lane = lax.broadcasted_iota(I32, x.shape, 1)
    first = (lane % (2 * half)) < half
    partner = jnp.where(first, pltpu.roll(x, n - half, 1), pltpu.roll(x, half, 1))
    return x * cos + partner * sin_signed


def _head_rms(x, seg_ref, g):
    x2 = x * x
    hi = x2.astype(BF16)
    lo = (x2 - hi.astype(F32)).astype(BF16)
    ms = (jnp.dot(hi, seg_ref[...], preferred_element_type=F32)
          + jnp.dot(lo, seg_ref[...], preferred_element_type=F32))
    return x * lax.rsqrt(ms + EPS) * g


def _proj_body(x_ref, g_ref, w_ref, gq_ref, gk_ref, seg_ref, cqk_ref, sqk_ref, ci_ref, si_ref,
               u_ref, k_ref, v_ref, ki_ref, qb_ref, kb_ref, vb_ref, qib_ref, ki8b_ref, wi_ref):
    h = _rms(x_ref[...], g_ref[...]).astype(BF16)
    p = jnp.dot(h, w_ref[...], preferred_element_type=F32)
    c0 = 0
    a_glu = p[:, c0:c0 + D_CONV]; c0 += D_CONV
    b_glu = p[:, c0:c0 + D_CONV]; c0 += D_CONV
    q = p[:, c0:c0 + D_ATTN]; c0 += D_ATTN
    k = p[:, c0:c0 + D_ATTN]; c0 += D_ATTN
    v = p[:, c0:c0 + D_ATTN]; c0 += D_ATTN
    qi = p[:, c0:c0 + D_IDX]; c0 += D_IDX
    ki8 = p[:, c0:c0 + D_IDX]; c0 += D_IDX
    wi = p[:, c0:c0 + LANES]

    u_ref[...] = a_glu * jax.nn.sigmoid(b_glu)
    q = _rope(_head_rms(q, seg_ref, gq_ref[...]), cqk_ref[...], sqk_ref[...], HEAD_DIM // 2)
    k = _rope(_head_rms(k, seg_ref, gk_ref[...]), cqk_ref[...], sqk_ref[...], HEAD_DIM // 2)
    qi = _rope(qi, ci_ref[...], si_ref[...], IDX_DIM // 2)
    ki8 = _rope(ki8, ci_ref[...], si_ref[...], IDX_DIM // 2)

    k_ref[...] = k
    v_ref[...] = v
    ki_ref[...] = ki8[:, :IDX_DIM]
    qb_ref[...] = (q * (HEAD_DIM ** -0.5)).astype(BF16)
    kb_ref[...] = k.astype(BF16)
    vb_ref[...] = v.astype(BF16)
    qib_ref[...] = qi.astype(BF16)
    ki8b_ref[...] = ki8.astype(BF16)
    wi_ref[...] = wi * IDX_W_SCALE


def _proj(x, g, w_bf, gq, gk, seg, tables, tab_rows):
    n, d = x.shape
    tm = _row_tile(math.gcd(n, tab_rows), 256)
    tab_blocks = tab_rows // tm
    row = lambda w: pl.BlockSpec((tm, w), lambda i: (i, 0))
    tab = lambda w: pl.BlockSpec((tm, w), lambda i: (i % tab_blocks, 0))
    f32o = lambda w: jax.ShapeDtypeStruct((n, w), F32)
    bfo = lambda w: jax.ShapeDtypeStruct((n, w), BF16)
    return pl.pallas_call(
        _proj_body,
        grid=(n // tm,),
        in_specs=[row(d), _const_spec((1, d)), _const_spec(w_bf.shape),
                  _const_spec((1, D_ATTN)), _const_spec((1, D_ATTN)), _const_spec((D_ATTN, D_ATTN)),
                  tab(D_ATTN), tab(D_ATTN), tab(D_IDX), tab(D_IDX)],
        out_specs=[row(D_CONV), row(D_ATTN), row(D_ATTN), row(IDX_DIM),
                   row(D_ATTN), row(D_ATTN), row(D_ATTN), row(D_IDX), row(D_IDX), row(LANES)],
        out_shape=[f32o(D_CONV), f32o(D_ATTN), f32o(D_ATTN), f32o(IDX_DIM),
                   bfo(D_ATTN), bfo(D_ATTN), bfo(D_ATTN), bfo(D_IDX), bfo(D_IDX), f32o(LANES)],
        compiler_params=pltpu.CompilerParams(dimension_semantics=("parallel",), vmem_limit_bytes=VMEM_LIMIT),
        name="proj",
    )(x, g, w_bf, gq, gk, seg, *tables)


def _conv_body(u_ref, uprev_ref, hist_ref, w_ref, b_ref, g_ref, bl_ref, o_ref, xs_ref, sh_ref, *, tc, n_t):
    xs_ref[HALO:HALO + tc, :] = u_ref[0]
    xs_ref[HALO + tc:HALO + tc + SUBLANES, :] = jnp.zeros((SUBLANES, D_CONV), F32)
    if n_t == 1:
        xs_ref[0:HALO, :] = hist_ref[0]
    else:
        i = pl.program_id(1)

        @pl.when(i == 0)
        def _():
            xs_ref[0:HALO, :] = hist_ref[0]

        @pl.when(i > 0)
        def _():
            xs_ref[0:HALO, :] = uprev_ref[0, tc - HALO:tc, :]

    n_sh = sh_ref.shape[1]
    for s in range(SUBLANES):
        sh_ref[s] = xs_ref[s:s + n_sh, :]

    rb = min(tc, 32)
    lead = HALO - (CONV_W - 1)
    for r0 in range(0, tc, rb):
        acc = jnp.broadcast_to(b_ref[...], (rb, D_CONV))
        for j in range(CONV_W):
            s = (lead + j) % SUBLANES
            base = r0 + lead + j - s
            acc = acc + w_ref[j:j + 1, :] * sh_ref[s, base:base + rb, :]
        mu = jnp.mean(acc, axis=-1, keepdims=True)
        yc = acc - mu
        y = yc * lax.rsqrt(jnp.mean(yc * yc, axis=-1, keepdims=True) + EPS) * g_ref[...] + bl_ref[...]
        o_ref[0, r0:r0 + rb, :] = y * jax.nn.sigmoid(y)


def _conv(u, hist, w_dw, b_dw, g_ln, b_ln):
    b, t, c = u.shape
    tc = _row_tile(t, 256)
    n_t = t // tc
    assert n_t == 1 or tc >= HALO
    vec = _const_spec((1, c))
    return pl.pallas_call(
        functools.partial(_conv_body, tc=tc, n_t=n_t),
        grid=(b, n_t),
        in_specs=[pl.BlockSpec((1, tc, c), lambda bi, i: (bi, i, 0)),
                  pl.BlockSpec((1, tc, c), lambda bi, i: (bi, jnp.maximum(i - 1, 0), 0)),
                  pl.BlockSpec((1, HALO, c), lambda bi, i: (bi, 0, 0)),
                  _const_spec((CONV_W, c)), vec, vec, vec],
        out_specs=pl.BlockSpec((1, tc, c), lambda bi, i: (bi, i, 0)),
        out_shape=jax.ShapeDtypeStruct((b, t, c), F32),
        scratch_shapes=[pltpu.VMEM((HALO + tc + SUBLANES, c), F32),
                        pltpu.VMEM((SUBLANES, HALO + tc, c), F32)],
        compiler_params=pltpu.CompilerParams(dimension_semantics=("parallel", "parallel")),
        name="conv",
    )(u, u, hist, w_dw, b_dw, g_ln, b_ln)


def _rank_to_f32(u):
    key = u + NEG_INF_KEY
    return lax.bitcast_convert_type(key ^ ((key >> 31) & 0x7FFFFFFF), F32)


def _count(mask):
    return jnp.sum(jnp.where(mask, 1.0, 0.0), axis=1, keepdims=True)


def _select_topk_bias(sc_ref, code_ref, admissible, col, topk, n_idx_bits):
    rows = sc_ref.shape[0]

    def value_step(i, u):
        cand = u + lax.shift_left(jnp.int32(1), 31 - i)
        return jnp.where(_count(sc_ref[...] >= _rank_to_f32(cand)) >= topk, cand, u)

    thr = _rank_to_f32(lax.fori_loop(0, 32, value_step, jnp.zeros((rows, 1), I32)))

    sc = sc_ref[...]
    code_ref[...] = jnp.where(sc > thr, -1, jnp.where(sc == thr, col, 2 ** n_idx_bits))

    def index_step(i, bound):
        cand = bound + lax.shift_left(jnp.int32(1), n_idx_bits - 1 - i)
        return jnp.where(_count(code_ref[...] < cand) < topk, cand, bound)

    bound = lax.fori_loop(0, n_idx_bits, index_step, jnp.zeros((rows, 1), I32))
    sc_ref[...] = jnp.where(admissible & (code_ref[...] <= bound), 0.0, MASKED_LOGIT)


def _attn_prompt_body(q_ref, k_ref, v_ref, qi_ref, ki8_ref, wi_ref, o_ref, bias_ref, code_ref, *, tq, q0, topk):
    t = q0 + tq
    qi = qi_ref[0]
    ki8 = ki8_ref[0]
    wi = wi_ref[0]
    idx_head = lax.broadcasted_iota(I32, (1, D_IDX), 1) // IDX_DIM
    scores = jnp.zeros((tq, t), F32)
    for h in range(N_IDX_HEADS):
        qh = qi * jnp.where(idx_head == h, 1.0, 0.0).astype(BF16)
        dots = lax.dot_general(qh, ki8, NT_DIMS, preferred_element_type=F32)
        scores = scores + wi[:, h:h + 1] * jnp.maximum(dots, 0.0)

    col = lax.broadcasted_iota(I32, (tq, t), 1)
    qpos = q0 + lax.broadcasted_iota(I32, (tq, t), 0)
    admissible = col <= qpos
    bias_ref[...] = jnp.where(admissible, scores, -jnp.inf)
    _select_topk_bias(bias_ref, code_ref, admissible, col, topk, max(1, (t - 1).bit_length()))

    pair_head = lax.broadcasted_iota(I32, (1, LANES), 1) // HEAD_DIM
    for j in range(D_ATTN // LANES):
        sl = slice(j * LANES, (j + 1) * LANES)
        qj = q_ref[0, :, sl]
        kj = k_ref[0, :, sl]
        vj = v_ref[0, :, sl]
        out = jnp.zeros((tq, LANES), F32)
        for half in range(LANES // HEAD_DIM):
            mine = pair_head == half
            qh = qj * jnp.where(mine, 1.0, 0.0).astype(BF16)
            logits = lax.dot_general(qh, kj, NT_DIMS, preferred_element_type=F32) + bias_ref[...]
            m = jnp.max(logits, axis=1, keepdims=True)
            p = jnp.exp(logits - m)
            denom = jnp.sum(p, axis=1, keepdims=True)
            o = jnp.dot(p.astype(BF16), vj, preferred_element_type=F32) / denom
            out = jnp.where(mine, o, out)
        o_ref[0, :, sl] = out


def _attn_prompt(q_bf, k_bf, v_bf, qi_bf, ki8_bf, wi):
    b, t, _ = q_bf.shape
    tq = _row_tile(t, 256)
    topk = min(TOPK_MAX, t // 4)
    blocks = []
    for qb in range(t // tq):
        kv = (qb + 1) * tq
        qspec = lambda w, qb=qb: pl.BlockSpec((1, tq, w), lambda bi: (bi, qb, 0))
        kspec = lambda w, kv=kv: pl.BlockSpec((1, kv, w), lambda bi: (bi, 0, 0))
        blocks.append(pl.pallas_call(
            functools.partial(_attn_prompt_body, tq=tq, q0=qb * tq, topk=topk),
            grid=(b,),
            in_specs=[qspec(D_ATTN), kspec(D_ATTN), kspec(D_ATTN), qspec(D_IDX), kspec(D_IDX), qspec(LANES)],
            out_specs=pl.BlockSpec((1, tq, D_ATTN), lambda bi: (bi, 0, 0)),
            out_shape=jax.ShapeDtypeStruct((b, tq, D_ATTN), F32),
            scratch_shapes=[pltpu.VMEM((tq, kv), F32), pltpu.VMEM((tq, kv), I32)],
            compiler_params=pltpu.CompilerParams(dimension_semantics=("parallel",), vmem_limit_bytes=VMEM_LIMIT),
            name=f"attn_prompt_q{qb}",
        )(q_bf, k_bf, v_bf, qi_bf, ki8_bf, wi))
    return jnp.concatenate(blocks, axis=1)


def _attn_sample_body(pt_ref, q_ref, knt_ref, vnt_ref, qi_ref, kint_ref, wrow_ref, ck_hbm, cv_hbm, cik_hbm,
                      o_ref, ikbuf, kbuf, vbuf, bias_ref, code_ref, sem_ik, sem_k, sem_v,
                      *, layer, n_pages, ts, pc_ik, pc_kv, topk):
    seq = pl.program_id(0)
    n_past = n_pages * PAGE_SIZE
    s_pad = n_past + PAGE_SIZE
    rows = N_HEADS * ts

    def page_copy(hbm, buf, sem, chunk, i, slot, pc):
        return pltpu.make_async_copy(hbm.at[layer, pt_ref[seq, chunk * pc + i]], buf.at[slot, i], sem.at[slot])

    def start_chunk(hbm, buf, sem, chunk, slot, pc):
        for i in range(pc):
            page_copy(hbm, buf, sem, chunk, i, slot, pc).start()

    def wait_chunk(hbm, buf, sem, chunk, slot, pc):
        for i in range(pc):
            page_copy(hbm, buf, sem, chunk, i, slot, pc).wait()

    start_chunk(cik_hbm, ikbuf, sem_ik, 0, 0, pc_ik)
    start_chunk(ck_hbm, kbuf, sem_k, 0, 0, pc_kv)
    start_chunk(cv_hbm, vbuf, sem_v, 0, 0, pc_kv)

    qi = qi_ref[0]
    wrow = wrow_ref[0]

    def page_scores(keys_t_bf):
        r = jnp.maximum(jnp.dot(qi, keys_t_bf, preferred_element_type=F32), 0.0) * wrow
        s = r[0:ts]
        for h in range(1, N_IDX_HEADS):
            s = s + r[h * ts:(h + 1) * ts]
        return s

    n_ik_chunks = n_pages // pc_ik

    def ik_chunk(c, carry):
        slot = c % 2

        @pl.when(c + 1 < n_ik_chunks)
        def _():
            start_chunk(cik_hbm, ikbuf, sem_ik, c + 1, 1 - slot, pc_ik)

        wait_chunk(cik_hbm, ikbuf, sem_ik, c, slot, pc_ik)
        for i in range(pc_ik):
            off = pl.multiple_of((c * pc_ik + i) * PAGE_SIZE, PAGE_SIZE)
            bias_ref[:, pl.ds(off, PAGE_SIZE)] = page_scores(ikbuf[slot, i].astype(BF16))
        return carry

    lax.fori_loop(0, n_ik_chunks, ik_chunk, 0)
    bias_ref[:, n_past:s_pad] = page_scores(kint_ref[0])

    col = lax.broadcasted_iota(I32, (ts, s_pad), 1)
    tpos = lax.broadcasted_iota(I32, (ts, s_pad), 0)
    admissible = (col - n_past) <= tpos
    bias_ref[...] = jnp.where(admissible, bias_ref[...], -jnp.inf)
    _select_topk_bias(bias_ref, code_ref, admissible, col, topk, max(1, (s_pad - 1).bit_length()))

    qf = q_ref[0].astype(F32)
    row_head = lax.broadcasted_iota(I32, (rows, D_ATTN), 0) // ts
    lane_head = lax.broadcasted_iota(I32, (rows, D_ATTN), 1) // HEAD_DIM
    qbd = jnp.where(row_head == lane_head, jnp.concatenate([qf] * N_HEADS, axis=0), 0.0).astype(BF16)

    def attend(carry, pages_k, pages_v, bias):
        m, l, acc = carry
        n = len(pages_k) * PAGE_SIZE
        logits = jnp.concatenate([jnp.dot(qbd, kt, preferred_element_type=F32) for kt in pages_k], axis=1)
        logits = (logits.reshape(N_HEADS, ts, n) + bias[None]).reshape(rows, n)
        m_new = jnp.maximum(m, jnp.max(logits, axis=1, keepdims=True))
        alpha = jnp.exp(m - m_new)
        p = jnp.exp(logits - m_new)
        l = alpha * l + jnp.sum(p, axis=1, keepdims=True)
        p = p.astype(BF16)
        acc = alpha * acc
        for i, vt in enumerate(pages_v):
            acc = acc + lax.dot_general(p[:, i * PAGE_SIZE:(i + 1) * PAGE_SIZE], vt, NT_DIMS,
                                        preferred_element_type=F32)
        return m_new, l, acc

    n_kv_chunks = n_pages // pc_kv
    n_chunk_keys = pc_kv * PAGE_SIZE

    def kv_chunk(c, carry):
        slot = c % 2

        @pl.when(c + 1 < n_kv_chunks)
        def _():
            start_chunk(ck_hbm, kbuf, sem_k, c + 1, 1 - slot, pc_kv)
            start_chunk(cv_hbm, vbuf, sem_v, c + 1, 1 - slot, pc_kv)

        wait_chunk(ck_hbm, kbuf, sem_k, c, slot, pc_kv)
        wait_chunk(cv_hbm, vbuf, sem_v, c, slot, pc_kv)
        off = pl.multiple_of(c * n_chunk_keys, n_chunk_keys)
        return attend(carry, [kbuf[slot, i].astype(BF16) for i in range(pc_kv)],
                      [vbuf[slot, i].astype(BF16) for i in range(pc_kv)], bias_ref[:, pl.ds(off, n_chunk_keys)])

    init = (jnp.full((rows, 1), SOFTMAX_FLOOR, F32), jnp.zeros((rows, 1), F32), jnp.zeros((rows, D_ATTN), F32))
    carry = lax.fori_loop(0, n_kv_chunks, kv_chunk, init)
    _, l, acc = attend(carry, [knt_ref[0]], [vnt_ref[0]], bias_ref[:, n_past:s_pad])

    acc = acc / l
    out = jnp.zeros((ts, D_ATTN), F32)
    head_of_lane = lax.broadcasted_iota(I32, (ts, D_ATTN), 1) // HEAD_DIM
    for h in range(N_HEADS):
        out = jnp.where(head_of_lane == h, acc[h * ts:(h + 1) * ts], out)
    o_ref[0] = out


def _attn_sample(layer, page_table, q_bf, knt_bf, vnt_bf, qi_rows, kint_bf, wrow, cache_kt, cache_vt, cache_ikt):
    db, ts, _ = q_bf.shape
    n_pages = page_table.shape[1]
    assert ts == SUBLANES, "one query block of 8 sublanes per sequence"
    pc_ik = _row_tile(n_pages, 32)
    pc_kv = _row_tile(n_pages, 16)
    s_pad = (n_pages + 1) * PAGE_SIZE
    topk = min(TOPK_MAX, (n_pages * PAGE_SIZE + ts) // 4)
    rows = N_HEADS * ts
    per_seq = lambda r, w: pl.BlockSpec((1, r, w), lambda s, pt: (s, 0, 0))
    any_spec = pl.BlockSpec(memory_space=pl.ANY)
    grid_spec = pltpu.PrefetchScalarGridSpec(
        num_scalar_prefetch=1,
        grid=(db,),
        in_specs=[per_seq(ts, D_ATTN), per_seq(D_ATTN, PAGE_SIZE), per_seq(D_ATTN, PAGE_SIZE),
                  per_seq(rows, IDX_DIM), per_seq(IDX_DIM, PAGE_SIZE), per_seq(rows, 1),
                  any_spec, any_spec, any_spec],
        out_specs=per_seq(ts, D_ATTN),
        scratch_shapes=[pltpu.VMEM((2, pc_ik, IDX_DIM, PAGE_SIZE), F32),
                        pltpu.VMEM((2, pc_kv, D_ATTN, PAGE_SIZE), F32),
                        pltpu.VMEM((2, pc_kv, D_ATTN, PAGE_SIZE), F32),
                        pltpu.VMEM((ts, s_pad), F32), pltpu.VMEM((ts, s_pad), I32),
                        pltpu.SemaphoreType.DMA((2,)), pltpu.SemaphoreType.DMA((2,)),
                        pltpu.SemaphoreType.DMA((2,))])
    return pl.pallas_call(
        functools.partial(_attn_sample_body, layer=layer, n_pages=n_pages, ts=ts, pc_ik=pc_ik, pc_kv=pc_kv,
                          topk=topk),
        grid_spec=grid_spec,
        out_shape=jax.ShapeDtypeStruct((db, ts, D_ATTN), F32),
        compiler_params=pltpu.CompilerParams(dimension_semantics=("arbitrary",), vmem_limit_bytes=VMEM_LIMIT),
        name="attn_sample",
    )(page_table, q_bf, knt_bf, vnt_bf, qi_rows, kint_bf, wrow, cache_kt, cache_vt, cache_ikt)


def _merge_body(x_ref, cy_ref, att_ref, g_ref, wg_ref, wc_ref, wa_ref, wo_ref, o_ref, *, d):
    x = x_ref[...]
    h = _rms(x, g_ref[...]).astype(BF16)
    gates = jax.nn.sigmoid(jnp.dot(h, wg_ref[...], preferred_element_type=F32))
    conv_out = jnp.dot(cy_ref[...].astype(BF16), wc_ref[...], preferred_element_type=F32)
    a = jnp.dot(att_ref[...].astype(BF16), wa_ref[...], preferred_element_type=F32)
    mixed = (gates[:, :d] * conv_out + gates[:, d:] * a).astype(BF16)
    o_ref[...] = x + jnp.dot(mixed, wo_ref[...], preferred_element_type=F32)


def _merge(x, conv_y, att, g, wg_bf, wc_bf, wa_bf, wo_bf):
    n, d = x.shape
    tm = _row_tile(n, 512)
    row = lambda w: pl.BlockSpec((tm, w), lambda i: (i, 0))
    return pl.pallas_call(
        functools.partial(_merge_body, d=d),
        grid=(n // tm,),
        in_specs=[row(d), row(D_CONV), row(D_ATTN), _const_spec((1, d)), _const_spec(wg_bf.shape),
                  _const_spec(wc_bf.shape), _const_spec(wa_bf.shape), _const_spec(wo_bf.shape)],
        out_specs=row(d),
        out_shape=jax.ShapeDtypeStruct((n, d), F32),
        compiler_params=pltpu.CompilerParams(dimension_semantics=("parallel",), vmem_limit_bytes=VMEM_LIMIT),
        name="merge",
    )(x, conv_y, att, g, wg_bf, wc_bf, wa_bf, wo_bf)


def _rope_tables(pos, head_dim, n_heads):
    half = head_dim // 2
    inv = jnp.float32(ROPE_THETA) ** (-jnp.arange(half, dtype=F32) * (2.0 / head_dim))
    ang = pos.astype(F32)[:, None] * inv[None, :]
    cos = jnp.tile(jnp.concatenate([jnp.cos(ang), jnp.cos(ang)], axis=1), (1, n_heads))
    sin = jnp.tile(jnp.concatenate([-jnp.sin(ang), jnp.sin(ang)], axis=1), (1, n_heads))
    return cos, sin


def kernel(x_prompt, x_sample, cache_k, cache_v, cache_idx_k, state_conv, page_table, g_ffn1, w_ffn1_in, w_ffn1_out, g_mix, w_in, g_q, g_k, w_dw, b_dw, g_cln, b_cln, w_cout, w_aout, w_o, g_ffn2, w_ffn2_in, w_ffn2_out):
    b, t, d = x_prompt.shape
    db, ts, _ = x_sample.shape
    depth = g_mix.shape[0]
    n_pages = page_table.shape[1]
    n_phys = cache_k.shape[1]
    past_len = n_pages * PAGE_SIZE
    hist_rows = CONV_W - 1

    pos_p = jnp.arange(t)
    pos_s = jnp.tile(past_len + jnp.arange(ts), db)
    tab_p = _rope_tables(pos_p, HEAD_DIM, N_HEADS) + _rope_tables(pos_p, IDX_DIM, N_IDX_HEADS)
    tab_s = _rope_tables(pos_s, HEAD_DIM, N_HEADS) + _rope_tables(pos_s, IDX_DIM, N_IDX_HEADS)
    seg = jnp.kron(jnp.eye(N_HEADS, dtype=F32), jnp.full((HEAD_DIM, HEAD_DIM), 1.0 / HEAD_DIM, F32)).astype(BF16)

    xp = x_prompt.reshape(b * t, d)
    xs = x_sample.reshape(db * ts, d)
    outs = [[] for _ in range(8)]
    for l in range(depth):
        vec = lambda a: a[l][None, :]
        w1i, w1o = w_ffn1_in[l].astype(BF16), w_ffn1_out[l].astype(BF16)
        w2i, w2o = w_ffn2_in[l].astype(BF16), w_ffn2_out[l].astype(BF16)
        c_main = 2 * D_CONV + 3 * D_ATTN + D_IDX
        w_ki = w_in[l][:, c_main:c_main + IDX_DIM]
        w_wi = w_in[l][:, c_main + IDX_DIM:c_main + IDX_DIM + N_IDX_HEADS]
        w_proj = jnp.concatenate([w_in[l][:, :c_main], jnp.tile(w_ki, (1, N_IDX_HEADS)),
                                  jnp.pad(w_wi, ((0, 0), (0, LANES - N_IDX_HEADS)))], axis=1).astype(BF16)
        w_gates = w_in[l][:, c_main + IDX_DIM + N_IDX_HEADS:].astype(BF16)
        gq = jnp.tile(g_q[l], N_HEADS)[None, :]
        gk = jnp.tile(g_k[l], N_HEADS)[None, :]
        wc, wa, wo = w_cout[l].astype(BF16), w_aout[l].astype(BF16), w_o[l].astype(BF16)

        xp = _ffn(xp, vec(g_ffn1), w1i, w1o)
        xs = _ffn(xs, vec(g_ffn1), w1i, w1o)
        up, kp, vp, kip, qbp, kbp, vbp, qibp, ki8bp, wip = _proj(xp, vec(g_mix), w_proj, gq, gk, seg, tab_p, t)
        us, ks, vs, kis, qbs, kbs, vbs, qibs, ki8bs, wis = _proj(xs, vec(g_mix), w_proj, gq, gk, seg, tab_s, db * ts)

        up3, us3 = up.reshape(b, t, D_CONV), us.reshape(db, ts, D_CONV)
        lead = ((0, 0), (HALO - hist_rows, 0), (0, 0))
        hist_p = jnp.zeros((b, HALO, D_CONV), F32)
        hist_s = jnp.pad(state_conv[l], lead)
        conv_args = (w_dw[l], vec(b_dw), vec(g_cln), vec(b_cln))
        cyp = _conv(up3, hist_p, *conv_args)
        cys = _conv(us3, hist_s, *conv_args)
        st_p = jnp.concatenate([jnp.zeros((b, hist_rows, D_CONV), F32), up3], axis=1)[:, -hist_rows:]
        st_s = jnp.concatenate([state_conv[l], us3], axis=1)[:, -hist_rows:]

        r3 = lambda a, n, w: a.reshape(n, -1, w)
        att_p = _attn_prompt(r3(qbp, b, D_ATTN), r3(kbp, b, D_ATTN), r3(vbp, b, D_ATTN),
                             r3(qibp, b, D_IDX), r3(ki8bp, b, D_IDX), r3(wip, b, LANES))
        head_major = lambda a, nh, w: a.reshape(db, ts, nh, w).transpose(0, 2, 1, 3).reshape(db, nh * ts, w)
        new_page_t = lambda a: jnp.pad(a.reshape(db, ts, -1).transpose(0, 2, 1), ((0, 0), (0, 0), (0, PAGE_SIZE - ts)))
        att_s = _attn_sample(
            l, page_table, r3(qbs, db, D_ATTN), new_page_t(kbs), new_page_t(vbs),
            head_major(qibs, N_IDX_HEADS, IDX_DIM), new_page_t(ki8bs[:, :IDX_DIM]),
            head_major(wis[:, :N_IDX_HEADS], N_IDX_HEADS, 1),
            cache_k.transpose(0, 1, 3, 4, 2).reshape(depth, n_phys, D_ATTN, PAGE_SIZE),
            cache_v.transpose(0, 1, 3, 4, 2).reshape(depth, n_phys, D_ATTN, PAGE_SIZE),
            cache_idx_k.transpose(0, 1, 3, 2))

        xp = _merge(xp, cyp.reshape(b * t, D_CONV), att_p.reshape(b * t, D_ATTN), vec(g_mix), w_gates, wc, wa, wo)
        xs = _merge(xs, cys.reshape(db * ts, D_CONV), att_s.reshape(db * ts, D_ATTN), vec(g_mix), w_gates, wc, wa, wo)
        xp = _ffn(xp, vec(g_ffn2), w2i, w2o)
        xs = _ffn(xs, vec(g_ffn2), w2i, w2o)

        for lst, val in zip(outs, (kp.reshape(b, t, N_HEADS, HEAD_DIM), vp.reshape(b, t, N_HEADS, HEAD_DIM),
                                   kip.reshape(b, t, IDX_DIM), st_p,
                                   ks.reshape(db, ts, N_HEADS, HEAD_DIM), vs.reshape(db, ts, N_HEADS, HEAD_DIM),
                                   kis.reshape(db, ts, IDX_DIM), st_s)):
            lst.append(val)

    return (xp.reshape(b, t, d), xs.reshape(db, ts, d)) + tuple(jnp.stack(o) for o in outs)
```

```python
import functools
import math

import jax
import jax.numpy as jnp
from jax import lax
from jax.experimental import pallas as pl
from jax.experimental.pallas import tpu as pltpu

N_HEADS = 8
HEAD_DIM = 64
D_ATTN = N_HEADS * HEAD_DIM
D_CONV = 512
CONV_W = 31
N_IDX_HEADS = 8
IDX_DIM = 32
D_IDX = N_IDX_HEADS * IDX_DIM
TOPK_MAX = 256
PAGE_SIZE = 128
ROPE_THETA = 10000.0
EPS = 1e-6
IDX_W_SCALE = D_IDX ** -0.5

LANES = 128
SUBLANES = 8
HALO = 32
NEG_INF_KEY = -2 ** 31 + 0x7FFFFF
MASKED_LOGIT = -1e30
SOFTMAX_FLOOR = 0.25 * MASKED_LOGIT
COUNT_FOLD = 16 * LANES
VMEM_LIMIT = 56 * 1024 * 1024

F32 = jnp.float32
BF16 = jnp.bfloat16
I32 = jnp.int32
NT_DIMS = (((1,), (1,)), ((), ()))


def _row_tile(n, cap):
    t = cap
    while n % t:
        t //= 2
    return t


def _rms(x, g):
    return x * lax.rsqrt(jnp.mean(x * x, axis=-1, keepdims=True) + EPS) * g


def _const_spec(shape):
    nd = len(shape)
    return pl.BlockSpec(shape, lambda *_: (0,) * nd, pipeline_mode=pl.Buffered(1))


def _ffn_body(x_ref, g_ref, win_ref, wout_ref, o_ref, *, d_ff, n_chunks):
    x = x_ref[...]
    h = _rms(x, g_ref[...]).astype(BF16)
    fc = d_ff // n_chunks
    acc = jnp.zeros_like(x)
    for c in range(n_chunks):
        gate = jnp.dot(h, win_ref[:, c * fc:(c + 1) * fc], preferred_element_type=F32)
        up = jnp.dot(h, win_ref[:, d_ff + c * fc:d_ff + (c + 1) * fc], preferred_element_type=F32)
        act = (gate * jax.nn.sigmoid(gate) * up).astype(BF16)
        acc = acc + jnp.dot(act, wout_ref[c * fc:(c + 1) * fc, :], preferred_element_type=F32)
    o_ref[...] = x + 0.5 * acc


def _ffn(x, g, w_in_bf, w_out_bf):
    n, d = x.shape
    d_ff = w_out_bf.shape[0]
    tm = _row_tile(n, 512)
    n_chunks = 2 if d_ff % (2 * LANES) == 0 else 1
    return pl.pallas_call(
        functools.partial(_ffn_body, d_ff=d_ff, n_chunks=n_chunks),
        grid=(n // tm,),
        in_specs=[pl.BlockSpec((tm, d), lambda i: (i, 0)),
                  _const_spec((1, d)), _const_spec((d, 2 * d_ff)), _const_spec((d_ff, d))],
        out_specs=pl.BlockSpec((tm, d), lambda i: (i, 0)),
        out_shape=jax.ShapeDtypeStruct((n, d), F32),
        compiler_params=pltpu.CompilerParams(dimension_semantics=("parallel",), vmem_limit_bytes=VMEM_LIMIT),
        name="ffn",
    )(x, g, w_in_bf, w_out_bf)


def _rope(x, cos, sin_signed, half):
    n = x.shape[-1]
    lane = lax.broadcasted_iota(I32, x.shape, 1)
    first = (lane % (2 * half)) < half
    partner = jnp.where(first, pltpu.roll(x, n - half, 1), pltpu.roll(x, half, 1))
    return x * cos + partner * sin_signed


def _head_rms(x, seg_ref, g):
    x2 = x * x
    hi = x2.astype(BF16)
    lo = (x2 - hi.astype(F32)).astype(BF16)
    ms = (jnp.dot(hi, seg_ref[...], preferred_element_type=F32)
          + jnp.dot(lo, seg_ref[...], preferred_element_type=F32))
    return x * lax.rsqrt(ms + EPS) * g


def _proj_body(x_ref, g_ref, w_ref, gq_ref, gk_ref, seg_ref, cqk_ref, sqk_ref, ci_ref, si_ref,
               u_ref, k_ref, v_ref, ki_ref, qb_ref, kb_ref, vb_ref, qib_ref, ki8b_ref, wi_ref, *, feature_major):
    h = _rms(x_ref[...], g_ref[...]).astype(BF16)
    p = jnp.dot(h, w_ref[...], preferred_element_type=F32)
    c0 = 0
    a_glu = p[:, c0:c0 + D_CONV]; c0 += D_CONV
    b_glu = p[:, c0:c0 + D_CONV]; c0 += D_CONV
    q = p[:, c0:c0 + D_ATTN]; c0 += D_ATTN
    k = p[:, c0:c0 + D_ATTN]; c0 += D_ATTN
    v = p[:, c0:c0 + D_ATTN]; c0 += D_ATTN
    qi = p[:, c0:c0 + D_IDX]; c0 += D_IDX
    ki8 = p[:, c0:c0 + D_IDX]; c0 += D_IDX
    wi = p[:, c0:c0 + LANES]

    u_ref[...] = a_glu * jax.nn.sigmoid(b_glu)
    q = _rope(_head_rms(q, seg_ref, gq_ref[...]), cqk_ref[...], sqk_ref[...], HEAD_DIM // 2)
    k = _rope(_head_rms(k, seg_ref, gk_ref[...]), cqk_ref[...], sqk_ref[...], HEAD_DIM // 2)
    qi = _rope(qi, ci_ref[...], si_ref[...], IDX_DIM // 2)
    ki8 = _rope(ki8, ci_ref[...], si_ref[...], IDX_DIM // 2)

    if feature_major:
        k_ref[0] = k.T
        v_ref[0] = v.T
        ki_ref[0] = ki8.T[:IDX_DIM, :]
    else:
        k_ref[...] = k
        v_ref[...] = v
        ki_ref[...] = ki8[:, :IDX_DIM]
    qb_ref[...] = (q * (HEAD_DIM ** -0.5)).astype(BF16)
    kb_ref[...] = k.astype(BF16)
    vb_ref[...] = v.astype(BF16)
    qib_ref[...] = qi.astype(BF16)
    ki8b_ref[...] = ki8.astype(BF16)
    wi_ref[...] = wi * IDX_W_SCALE


def _proj(x, g, w_bf, gq, gk, seg, tables, tab_rows, feature_major):
    n, d = x.shape
    tm = _row_tile(math.gcd(n, tab_rows), 256)
    tab_blocks = tab_rows // tm
    row = lambda w: pl.BlockSpec((tm, w), lambda i: (i, 0))
    tab = lambda w: pl.BlockSpec((tm, w), lambda i: (i % tab_blocks, 0))
    f32o = lambda w: jax.ShapeDtypeStruct((n, w), F32)
    bfo = lambda w: jax.ShapeDtypeStruct((n, w), BF16)
    if feature_major:
        cache = lambda w: pl.BlockSpec((1, w, tm), lambda i: (i // tab_blocks, 0, i % tab_blocks))
        cacheo = lambda w: jax.ShapeDtypeStruct((n // tab_rows, w, tab_rows), F32)
    else:
        cache, cacheo = row, f32o
    return pl.pallas_call(
        functools.partial(_proj_body, feature_major=feature_major),
        grid=(n // tm,),
        in_specs=[row(d), _const_spec((1, d)), _const_spec(w_bf.shape),
                  _const_spec((1, D_ATTN)), _const_spec((1, D_ATTN)), _const_spec((D_ATTN, D_ATTN)),
                  tab(D_ATTN), tab(D_ATTN), tab(D_IDX), tab(D_IDX)],
        out_specs=[row(D_CONV), cache(D_ATTN), cache(D_ATTN), cache(IDX_DIM),
                   row(D_ATTN), row(D_ATTN), row(D_ATTN), row(D_IDX), row(D_IDX), row(LANES)],
        out_shape=[f32o(D_CONV), cacheo(D_ATTN), cacheo(D_ATTN), cacheo(IDX_DIM),
                   bfo(D_ATTN), bfo(D_ATTN), bfo(D_ATTN), bfo(D_IDX), bfo(D_IDX), f32o(LANES)],
        compiler_params=pltpu.CompilerParams(dimension_semantics=("parallel",), vmem_limit_bytes=VMEM_LIMIT),
        name="proj",
    )(x, g, w_bf, gq, gk, seg, *tables)


def _conv_body(u_ref, uprev_ref, hist_ref, w_ref, b_ref, g_ref, bl_ref, o_ref, xs_ref, sh_ref, *, tc, n_t):
    xs_ref[HALO:HALO + tc, :] = u_ref[0]
    xs_ref[HALO + tc:HALO + tc + SUBLANES, :] = jnp.zeros((SUBLANES, D_CONV), F32)
    if n_t == 1:
        xs_ref[0:HALO, :] = hist_ref[0]
    else:
        i = pl.program_id(1)

        @pl.when(i == 0)
        def _():
            xs_ref[0:HALO, :] = hist_ref[0]

        @pl.when(i > 0)
        def _():
            xs_ref[0:HALO, :] = uprev_ref[0, tc - HALO:tc, :]

    n_sh = sh_ref.shape[1]
    for s in range(SUBLANES):
        sh_ref[s] = xs_ref[s:s + n_sh, :]

    rb = min(tc, 32)
    lead = HALO - (CONV_W - 1)
    for r0 in range(0, tc, rb):
        acc = jnp.broadcast_to(b_ref[...], (rb, D_CONV))
        for j in range(CONV_W):
            s = (lead + j) % SUBLANES
            base = r0 + lead + j - s
            acc = acc + w_ref[j:j + 1, :] * sh_ref[s, base:base + rb, :]
        mu = jnp.mean(acc, axis=-1, keepdims=True)
        yc = acc - mu
        y = yc * lax.rsqrt(jnp.mean(yc * yc, axis=-1, keepdims=True) + EPS) * g_ref[...] + bl_ref[...]
        o_ref[0, r0:r0 + rb, :] = y * jax.nn.sigmoid(y)


def _conv(u, hist, w_dw, b_dw, g_ln, b_ln):
    b, t, c = u.shape
    tc = _row_tile(t, 256)
    n_t = t // tc
    assert n_t == 1 or tc >= HALO
    vec = _const_spec((1, c))
    return pl.pallas_call(
        functools.partial(_conv_body, tc=tc, n_t=n_t),
        grid=(b, n_t),
        in_specs=[pl.BlockSpec((1, tc, c), lambda bi, i: (bi, i, 0)),
                  pl.BlockSpec((1, tc, c), lambda bi, i: (bi, jnp.maximum(i - 1, 0), 0)),
                  pl.BlockSpec((1, HALO, c), lambda bi, i: (bi, 0, 0)),
                  _const_spec((CONV_W, c)), vec, vec, vec],
        out_specs=pl.BlockSpec((1, tc, c), lambda bi, i: (bi, i, 0)),
        out_shape=jax.ShapeDtypeStruct((b, t, c), F32),
        scratch_shapes=[pltpu.VMEM((HALO + tc + SUBLANES, c), F32),
                        pltpu.VMEM((SUBLANES, HALO + tc, c), F32)],
        compiler_params=pltpu.CompilerParams(dimension_semantics=("parallel", "parallel")),
        name="conv",
    )(u, u, hist, w_dw, b_dw, g_ln, b_ln)


def _rank_to_f32(u):
    key = u + NEG_INF_KEY
    return lax.bitcast_convert_type(key ^ ((key >> 31) & 0x7FFFFFFF), F32)


def _count(mask):
    ones = jnp.where(mask, 1.0, 0.0)
    n = ones.shape[1]
    if n > COUNT_FOLD:
        parts = [ones[:, i:i + COUNT_FOLD] for i in range(0, n - n % COUNT_FOLD, COUNT_FOLD)]
        while len(parts) > 1:
            parts = [a + b for a, b in zip(parts[0::2], parts[1::2])] + parts[len(parts) & ~1:]
        total = jnp.sum(parts[0], axis=1, keepdims=True)
        if n % COUNT_FOLD:
            total = total + jnp.sum(ones[:, n - n % COUNT_FOLD:], axis=1, keepdims=True)
        return total
    return jnp.sum(ones, axis=1, keepdims=True)


def _select_topk_bias(sc_ref, code_ref, admissible, col, topk, n_idx_bits, bits_per_step):
    rows = sc_ref.shape[0]
    n_cand = 2 ** bits_per_step - 1
    assert 32 % bits_per_step == 0
    n_idx_bits = -(-n_idx_bits // bits_per_step) * bits_per_step

    def value_step(i, u):
        unit = lax.shift_left(jnp.int32(1), 32 - bits_per_step * (i + 1))
        sc = sc_ref[...]
        taken = jnp.zeros((rows, 1), I32)
        for j in range(1, n_cand + 1):
            taken = taken + jnp.where(_count(sc >= _rank_to_f32(u + j * unit)) >= topk, 1, 0)
        return u + taken * unit

    thr = _rank_to_f32(lax.fori_loop(0, 32 // bits_per_step, value_step, jnp.zeros((rows, 1), I32)))

    sc = sc_ref[...]
    code_ref[...] = jnp.where(sc > thr, -1, jnp.where(sc == thr, col, 2 ** n_idx_bits))

    def index_step(i, bound):
        unit = lax.shift_left(jnp.int32(1), n_idx_bits - bits_per_step * (i + 1))
        code = code_ref[...]
        taken = jnp.zeros((rows, 1), I32)
        for j in range(1, n_cand + 1):
            taken = taken + jnp.where(_count(code < bound + j * unit) < topk, 1, 0)
        return bound + taken * unit

    bound = lax.fori_loop(0, n_idx_bits // bits_per_step, index_step, jnp.zeros((rows, 1), I32))
    sc_ref[...] = jnp.where(admissible & (code_ref[...] <= bound), 0.0, MASKED_LOGIT)


def _attn_prompt_body(q_ref, k_ref, v_ref, qi_ref, ki8_ref, wi_ref, o_ref, bias_ref, code_ref, *, tq, q0, topk):
    t = q0 + tq
    qi = qi_ref[0]
    ki8 = ki8_ref[0]
    wi = wi_ref[0]
    idx_head = lax.broadcasted_iota(I32, (1, D_IDX), 1) // IDX_DIM
    scores = jnp.zeros((tq, t), F32)
    for h in range(N_IDX_HEADS):
        qh = qi * jnp.where(idx_head == h, 1.0, 0.0).astype(BF16)
        dots = lax.dot_general(qh, ki8, NT_DIMS, preferred_element_type=F32)
        scores = scores + wi[:, h:h + 1] * jnp.maximum(dots, 0.0)

    col = lax.broadcasted_iota(I32, (tq, t), 1)
    qpos = q0 + lax.broadcasted_iota(I32, (tq, t), 0)
    admissible = col <= qpos
    bias_ref[...] = jnp.where(admissible, scores, -jnp.inf)
    _select_topk_bias(bias_ref, code_ref, admissible, col, topk, max(1, (t - 1).bit_length()), bits_per_step=1)

    pair_head = lax.broadcasted_iota(I32, (1, LANES), 1) // HEAD_DIM
    for j in range(D_ATTN // LANES):
        sl = slice(j * LANES, (j + 1) * LANES)
        qj = q_ref[0, :, sl]
        kj = k_ref[0, :, sl]
        vj = v_ref[0, :, sl]
        out = jnp.zeros((tq, LANES), F32)
        for half in range(LANES // HEAD_DIM):
            mine = pair_head == half
            qh = qj * jnp.where(mine, 1.0, 0.0).astype(BF16)
            logits = lax.dot_general(qh, kj, NT_DIMS, preferred_element_type=F32) + bias_ref[...]
            m = jnp.max(logits, axis=1, keepdims=True)
            p = jnp.exp(logits - m)
            denom = jnp.sum(p, axis=1, keepdims=True)
            o = jnp.dot(p.astype(BF16), vj, preferred_element_type=F32) / denom
            out = jnp.where(mine, o, out)
        o_ref[0, :, sl] = out


def _attn_prompt(q_bf, k_bf, v_bf, qi_bf, ki8_bf, wi):
    b, t, _ = q_bf.shape
    tq = _row_tile(t, 256)
    topk = min(TOPK_MAX, t // 4)
    blocks = []
    for qb in range(t // tq):
        kv = (qb + 1) * tq
        qspec = lambda w, qb=qb: pl.BlockSpec((1, tq, w), lambda bi: (bi, qb, 0))
        kspec = lambda w, kv=kv: pl.BlockSpec((1, kv, w), lambda bi: (bi, 0, 0))
        blocks.append(pl.pallas_call(
            functools.partial(_attn_prompt_body, tq=tq, q0=qb * tq, topk=topk),
            grid=(b,),
            in_specs=[qspec(D_ATTN), kspec(D_ATTN), kspec(D_ATTN), qspec(D_IDX), kspec(D_IDX), qspec(LANES)],
            out_specs=pl.BlockSpec((1, tq, D_ATTN), lambda bi: (bi, 0, 0)),
            out_shape=jax.ShapeDtypeStruct((b, tq, D_ATTN), F32),
            scratch_shapes=[pltpu.VMEM((tq, kv), F32), pltpu.VMEM((tq, kv), I32)],
            compiler_params=pltpu.CompilerParams(dimension_semantics=("parallel",), vmem_limit_bytes=VMEM_LIMIT),
            name=f"attn_prompt_q{qb}",
        )(q_bf, k_bf, v_bf, qi_bf, ki8_bf, wi))
    return jnp.concatenate(blocks, axis=1)


def _attn_sample_body(pt_ref, q_ref, knt_ref, vnt_ref, qi_ref, kint_ref, wrow_ref, ck_hbm, cv_hbm, cik_hbm,
                      o_ref, ikbuf, kbuf, vbuf, bias_ref, code_ref, sem_ik, sem_k, sem_v,
                      *, layer, n_seq, n_pages, ts, pc_ik, pc_kv, kv_slots, topk):
    seq = pl.program_id(0)
    n_past = n_pages * PAGE_SIZE
    s_pad = n_past + PAGE_SIZE
    rows = N_HEADS * ts
    n_ik_chunks = n_pages // pc_ik
    n_kv_chunks = n_pages // pc_kv
    n_kv_total = n_seq * n_kv_chunks

    def page_copy(hbm, buf, sem, s, page, slot, i):
        return pltpu.make_async_copy(hbm.at[layer, pt_ref[s, page]], buf.at[slot, i], sem.at[slot])

    def ik_copies(s):
        return [page_copy(cik_hbm, ikbuf, sem_ik, s, p, s % 2, p) for p in range(n_pages)]

    def kv_copies(g):
        s, c = g // n_kv_chunks, g % n_kv_chunks
        slot = g % kv_slots
        return [page_copy(hbm, buf, sem, s, c * pc_kv + i, slot, i)
                for hbm, buf, sem in ((ck_hbm, kbuf, sem_k), (cv_hbm, vbuf, sem_v)) for i in range(pc_kv)]

    @pl.when(seq == 0)
    def _():
        for cp in ik_copies(0):
            cp.start()
        for g in range(min(kv_slots - 1, n_kv_total)):
            for cp in kv_copies(g):
                cp.start()

    @pl.when(seq + 1 < n_seq)
    def _():
        for cp in ik_copies(seq + 1):
            cp.start()

    qi = qi_ref[0]
    wrow = wrow_ref[0]

    def page_scores(keys_t_bf):
        r = jnp.maximum(jnp.dot(qi, keys_t_bf, preferred_element_type=F32), 0.0) * wrow
        s = r[0:ts]
        for h in range(1, N_IDX_HEADS):
            s = s + r[h * ts:(h + 1) * ts]
        return s

    for cp in ik_copies(seq):
        cp.wait()

    def ik_chunk(c, carry):
        for i in range(pc_ik):
            page = c * pc_ik + i
            off = pl.multiple_of(page * PAGE_SIZE, PAGE_SIZE)
            bias_ref[:, pl.ds(off, PAGE_SIZE)] = page_scores(ikbuf[seq % 2, page].astype(BF16))
        return carry

    lax.fori_loop(0, n_ik_chunks, ik_chunk, 0)
    bias_ref[:, n_past:s_pad] = page_scores(kint_ref[0])

    col = lax.broadcasted_iota(I32, (ts, s_pad), 1)
    tpos = lax.broadcasted_iota(I32, (ts, s_pad), 0)
    admissible = (col - n_past) <= tpos
    bias_ref[...] = jnp.where(admissible, bias_ref[...], -jnp.inf)
    _select_topk_bias(bias_ref, code_ref, admissible, col, topk, max(1, (s_pad - 1).bit_length()), bits_per_step=2)

    qf = q_ref[0].astype(F32)
    row_head = lax.broadcasted_iota(I32, (rows, D_ATTN), 0) // ts
    lane_head = lax.broadcasted_iota(I32, (rows, D_ATTN), 1) // HEAD_DIM
    qbd = jnp.where(row_head == lane_head, jnp.concatenate([qf] * N_HEADS, axis=0), 0.0).astype(BF16)

    def attend(carry, pages_k, pages_v, bias):
        m, l, acc = carry
        n = len(pages_k) * PAGE_SIZE
        logits = jnp.concatenate([jnp.dot(qbd, kt, preferred_element_type=F32) for kt in pages_k], axis=1)
        logits = (logits.reshape(N_HEADS, ts, n) + bias[None]).reshape(rows, n)
        m_new = jnp.maximum(m, jnp.max(logits, axis=1, keepdims=True))
        alpha = jnp.exp(m - m_new)
        p = jnp.exp(logits - m_new)
        l = alpha * l + jnp.sum(p, axis=1, keepdims=True)
        p = p.astype(BF16)
        acc = alpha * acc
        for i, vt in enumerate(pages_v):
            acc = acc + lax.dot_general(p[:, i * PAGE_SIZE:(i + 1) * PAGE_SIZE], vt, NT_DIMS,
                                        preferred_element_type=F32)
        return m_new, l, acc

    n_chunk_keys = pc_kv * PAGE_SIZE

    def kv_chunk(c, carry):
        g = seq * n_kv_chunks + c
        slot = g % kv_slots

        @pl.when(g + kv_slots - 1 < n_kv_total)
        def _():
            for cp in kv_copies(g + kv_slots - 1):
                cp.start()

        for cp in kv_copies(g):
            cp.wait()
        off = pl.multiple_of(c * n_chunk_keys, n_chunk_keys)
        return attend(carry, [kbuf[slot, i].astype(BF16) for i in range(pc_kv)],
                      [vbuf[slot, i].astype(BF16) for i in range(pc_kv)], bias_ref[:, pl.ds(off, n_chunk_keys)])

    init = (jnp.full((rows, 1), SOFTMAX_FLOOR, F32), jnp.zeros((rows, 1), F32), jnp.zeros((rows, D_ATTN), F32))
    carry = lax.fori_loop(0, n_kv_chunks, kv_chunk, init)
    _, l, acc = attend(carry, [knt_ref[0]], [vnt_ref[0]], bias_ref[:, n_past:s_pad])

    acc = acc / l
    out = jnp.zeros((ts, D_ATTN), F32)
    head_of_lane = lax.broadcasted_iota(I32, (ts, D_ATTN), 1) // HEAD_DIM
    for h in range(N_HEADS):
        out = jnp.where(head_of_lane == h, acc[h * ts:(h + 1) * ts], out)
    o_ref[0] = out


def _attn_sample(layer, page_table, q_bf, knt_bf, vnt_bf, qi_rows, kint_bf, wrow, cache_kt, cache_vt, cache_ikt):
    db, ts, _ = q_bf.shape
    n_pages = page_table.shape[1]
    assert ts == SUBLANES, "one query block of 8 sublanes per sequence"
    pc_ik = _row_tile(n_pages, 32)
    pc_kv = _row_tile(n_pages, 16)
    kv_slots = 4
    s_pad = (n_pages + 1) * PAGE_SIZE
    topk = min(TOPK_MAX, (n_pages * PAGE_SIZE + ts) // 4)
    rows = N_HEADS * ts
    per_seq = lambda r, w: pl.BlockSpec((1, r, w), lambda s, pt: (s, 0, 0))
    any_spec = pl.BlockSpec(memory_space=pl.ANY)
    grid_spec = pltpu.PrefetchScalarGridSpec(
        num_scalar_prefetch=1,
        grid=(db,),
        in_specs=[per_seq(ts, D_ATTN), per_seq(D_ATTN, PAGE_SIZE), per_seq(D_ATTN, PAGE_SIZE),
                  per_seq(rows, IDX_DIM), per_seq(IDX_DIM, PAGE_SIZE), per_seq(rows, 1),
                  any_spec, any_spec, any_spec],
        out_specs=per_seq(ts, D_ATTN),
        scratch_shapes=[pltpu.VMEM((2, n_pages, IDX_DIM, PAGE_SIZE), F32),
                        pltpu.VMEM((kv_slots, pc_kv, D_ATTN, PAGE_SIZE), F32),
                        pltpu.VMEM((kv_slots, pc_kv, D_ATTN, PAGE_SIZE), F32),
                        pltpu.VMEM((ts, s_pad), F32), pltpu.VMEM((ts, s_pad), I32),
                        pltpu.SemaphoreType.DMA((2,)), pltpu.SemaphoreType.DMA((kv_slots,)),
                        pltpu.SemaphoreType.DMA((kv_slots,))])
    return pl.pallas_call(
        functools.partial(_attn_sample_body, layer=layer, n_seq=db, n_pages=n_pages, ts=ts, pc_ik=pc_ik,
                          pc_kv=pc_kv, kv_slots=kv_slots, topk=topk),
        grid_spec=grid_spec,
        out_shape=jax.ShapeDtypeStruct((db, ts, D_ATTN), F32),
        compiler_params=pltpu.CompilerParams(dimension_semantics=("arbitrary",), vmem_limit_bytes=VMEM_LIMIT),
        name="attn_sample",
    )(page_table, q_bf, knt_bf, vnt_bf, qi_rows, kint_bf, wrow, cache_kt, cache_vt, cache_ikt)


def _merge_body(x_ref, cy_ref, att_ref, g_ref, wg_ref, wc_ref, wa_ref, wo_ref, o_ref, *, d):
    x = x_ref[...]
    h = _rms(x, g_ref[...]).astype(BF16)
    gates = jax.nn.sigmoid(jnp.dot(h, wg_ref[...], preferred_element_type=F32))
    conv_out = jnp.dot(cy_ref[...].astype(BF16), wc_ref[...], preferred_element_type=F32)
    a = jnp.dot(att_ref[...].astype(BF16), wa_ref[...], preferred_element_type=F32)
    mixed = (gates[:, :d] * conv_out + gates[:, d:] * a).astype(BF16)
    o_ref[...] = x + jnp.dot(mixed, wo_ref[...], preferred_element_type=F32)


def _merge(x, conv_y, att, g, wg_bf, wc_bf, wa_bf, wo_bf):
    n, d = x.shape
    tm = _row_tile(n, 512)
    row = lambda w: pl.BlockSpec((tm, w), lambda i: (i, 0))
    return pl.pallas_call(
        functools.partial(_merge_body, d=d),
        grid=(n // tm,),
        in_specs=[row(d), row(D_CONV), row(D_ATTN), _const_spec((1, d)), _const_spec(wg_bf.shape),
                  _const_spec(wc_bf.shape), _const_spec(wa_bf.shape), _const_spec(wo_bf.shape)],
        out_specs=row(d),
        out_shape=jax.ShapeDtypeStruct((n, d), F32),
        compiler_params=pltpu.CompilerParams(dimension_semantics=("parallel",), vmem_limit_bytes=VMEM_LIMIT),
        name="merge",
    )(x, conv_y, att, g, wg_bf, wc_bf, wa_bf, wo_bf)


def _rope_tables(pos, head_dim, n_heads):
    half = head_dim // 2
    inv = jnp.float32(ROPE_THETA) ** (-jnp.arange(half, dtype=F32) * (2.0 / head_dim))
    ang = pos.astype(F32)[:, None] * inv[None, :]
    cos = jnp.tile(jnp.concatenate([jnp.cos(ang), jnp.cos(ang)], axis=1), (1, n_heads))
    sin = jnp.tile(jnp.concatenate([-jnp.sin(ang), jnp.sin(ang)], axis=1), (1, n_heads))
    return cos, sin


def kernel(x_prompt, x_sample, cache_k, cache_v, cache_idx_k, state_conv, page_table, g_ffn1, w_ffn1_in, w_ffn1_out, g_mix, w_in, g_q, g_k, w_dw, b_dw, g_cln, b_cln, w_cout, w_aout, w_o, g_ffn2, w_ffn2_in, w_ffn2_out):
    b, t, d = x_prompt.shape
    db, ts, _ = x_sample.shape
    depth = g_mix.shape[0]
    n_pages = page_table.shape[1]
    n_phys = cache_k.shape[1]
    past_len = n_pages * PAGE_SIZE
    hist_rows = CONV_W - 1

    pos_p = jnp.arange(t)
    pos_s = jnp.tile(past_len + jnp.arange(ts), db)
    tab_p = _rope_tables(pos_p, HEAD_DIM, N_HEADS) + _rope_tables(pos_p, IDX_DIM, N_IDX_HEADS)
    tab_s = _rope_tables(pos_s, HEAD_DIM, N_HEADS) + _rope_tables(pos_s, IDX_DIM, N_IDX_HEADS)
    seg = jnp.kron(jnp.eye(N_HEADS, dtype=F32), jnp.full((HEAD_DIM, HEAD_DIM), 1.0 / HEAD_DIM, F32)).astype(BF16)

    xp = x_prompt.reshape(b * t, d)
    xs = x_sample.reshape(db * ts, d)
    outs = [[] for _ in range(8)]
    for l in range(depth):
        vec = lambda a: a[l][None, :]
        w1i, w1o = w_ffn1_in[l].astype(BF16), w_ffn1_out[l].astype(BF16)
        w2i, w2o = w_ffn2_in[l].astype(BF16), w_ffn2_out[l].astype(BF16)
        c_main = 2 * D_CONV + 3 * D_ATTN + D_IDX
        w_ki = w_in[l][:, c_main:c_main + IDX_DIM]
        w_wi = w_in[l][:, c_main + IDX_DIM:c_main + IDX_DIM + N_IDX_HEADS]
        w_proj = jnp.concatenate([w_in[l][:, :c_main], jnp.tile(w_ki, (1, N_IDX_HEADS)),
                                  jnp.pad(w_wi, ((0, 0), (0, LANES - N_IDX_HEADS)))], axis=1).astype(BF16)
        w_gates = w_in[l][:, c_main + IDX_DIM + N_IDX_HEADS:].astype(BF16)
        gq = jnp.tile(g_q[l], N_HEADS)[None, :]
        gk = jnp.tile(g_k[l], N_HEADS)[None, :]
        wc, wa, wo = w_cout[l].astype(BF16), w_aout[l].astype(BF16), w_o[l].astype(BF16)

        xp = _ffn(xp, vec(g_ffn1), w1i, w1o)
        xs = _ffn(xs, vec(g_ffn1), w1i, w1o)
        up, kp, vp, kip, qbp, kbp, vbp, qibp, ki8bp, wip = _proj(xp, vec(g_mix), w_proj, gq, gk, seg, tab_p, t, True)
        us, ks, vs, kis, qbs, kbs, vbs, qibs, ki8bs, wis = _proj(xs, vec(g_mix), w_proj, gq, gk, seg, tab_s, db * ts,
                                                                 False)

        up3, us3 = up.reshape(b, t, D_CONV), us.reshape(db, ts, D_CONV)
        lead = ((0, 0), (HALO - hist_rows, 0), (0, 0))
        hist_p = jnp.zeros((b, HALO, D_CONV), F32)
        hist_s = jnp.pad(state_conv[l], lead)
        conv_args = (w_dw[l], vec(b_dw), vec(g_cln), vec(b_cln))
        cyp = _conv(up3, hist_p, *conv_args)
        cys = _conv(us3, hist_s, *conv_args)
        st_p = jnp.concatenate([jnp.zeros((b, hist_rows, D_CONV), F32), up3], axis=1)[:, -hist_rows:]
        st_s = jnp.concatenate([state_conv[l], us3], axis=1)[:, -hist_rows:]

        r3 = lambda a, n, w: a.reshape(n, -1, w)
        att_p = _attn_prompt(r3(qbp, b, D_ATTN), r3(kbp, b, D_ATTN), r3(vbp, b, D_ATTN),
                             r3(qibp, b, D_IDX), r3(ki8bp, b, D_IDX), r3(wip, b, LANES))
        head_major = lambda a, nh, w: a.reshape(db, ts, nh, w).transpose(0, 2, 1, 3).reshape(db, nh * ts, w)
        new_page_t = lambda a: jnp.pad(a.reshape(db, ts, -1).transpose(0, 2, 1), ((0, 0), (0, 0), (0, PAGE_SIZE - ts)))
        att_s = _attn_sample(
            l, page_table, r3(qbs, db, D_ATTN), new_page_t(kbs), new_page_t(vbs),
            head_major(qibs, N_IDX_HEADS, IDX_DIM), new_page_t(ki8bs[:, :IDX_DIM]),
            head_major(wis[:, :N_IDX_HEADS], N_IDX_HEADS, 1),
            cache_k.transpose(0, 1, 3, 4, 2).reshape(depth, n_phys, D_ATTN, PAGE_SIZE),
            cache_v.transpose(0, 1, 3, 4, 2).reshape(depth, n_phys, D_ATTN, PAGE_SIZE),
            cache_idx_k.transpose(0, 1, 3, 2))

        xp = _merge(xp, cyp.reshape(b * t, D_CONV), att_p.reshape(b * t, D_ATTN), vec(g_mix), w_gates, wc, wa, wo)
        xs = _merge(xs, cys.reshape(db * ts, D_CONV), att_s.reshape(db * ts, D_ATTN), vec(g_mix), w_gates, wc, wa, wo)
        xp = _ffn(xp, vec(g_ffn2), w2i, w2o)
        xs = _ffn(xs, vec(g_ffn2), w2i, w2o)

        token_major = lambda a: a.reshape(b, N_HEADS, HEAD_DIM, t).transpose(0, 3, 1, 2)
        for lst, val in zip(outs, (token_major(kp), token_major(vp), kip.transpose(0, 2, 1), st_p,
                                   ks.reshape(db, ts, N_HEADS, HEAD_DIM), vs.reshape(db, ts, N_HEADS, HEAD_DIM),
                                   kis.reshape(db, ts, IDX_DIM), st_s)):
            lst.append(val)

    return (xp.reshape(b, t, d), xs.reshape(db, ts, d)) + tuple(jnp.stack(o) for o in outs)
```

```python
import functools
import math

import jax
import jax.numpy as jnp
from jax import lax
from jax.experimental import pallas as pl
from jax.experimental.pallas import tpu as pltpu

N_HEADS = 8
HEAD_DIM = 64
D_ATTN = N_HEADS * HEAD_DIM
D_CONV = 512
CONV_W = 31
N_IDX_HEADS = 8
IDX_DIM = 32
D_IDX = N_IDX_HEADS * IDX_DIM
TOPK_MAX = 256
PAGE_SIZE = 128
ROPE_THETA = 10000.0
EPS = 1e-6
IDX_W_SCALE = D_IDX ** -0.5

LANES = 128
SUBLANES = 8
HALO = 32
INT_MIN = -2 ** 31
NEG_INF_KEY = INT_MIN + 0x7FFFFF
MASKED_LOGIT = -1e30
SOFTMAX_FLOOR = 0.25 * MASKED_LOGIT
COUNT_FOLD = 16 * LANES
VMEM_LIMIT = 56 * 1024 * 1024

F32 = jnp.float32
BF16 = jnp.bfloat16
I32 = jnp.int32
NT_DIMS = (((1,), (1,)), ((), ()))


def _row_tile(n, cap):
    t = cap
    while n % t:
        t //= 2
    return t


def _rms(x, g):
    return x * lax.rsqrt(jnp.mean(x * x, axis=-1, keepdims=True) + EPS) * g


def _const_spec(shape):
    nd = len(shape)
    return pl.BlockSpec(shape, lambda *_: (0,) * nd, pipeline_mode=pl.Buffered(1))


def _ffn_body(x_ref, g_ref, win_ref, wout_ref, o_ref, *, d_ff, n_chunks):
    x = x_ref[...]
    h = _rms(x, g_ref[...]).astype(BF16)
    fc = d_ff // n_chunks
    acc = jnp.zeros_like(x)
    for c in range(n_chunks):
        gate = jnp.dot(h, win_ref[:, c * fc:(c + 1) * fc], preferred_element_type=F32)
        up = jnp.dot(h, win_ref[:, d_ff + c * fc:d_ff + (c + 1) * fc], preferred_element_type=F32)
        act = (gate * jax.nn.sigmoid(gate) * up).astype(BF16)
        acc = acc + jnp.dot(act, wout_ref[c * fc:(c + 1) * fc, :], preferred_element_type=F32)
    o_ref[...] = x + 0.5 * acc


def _ffn(x, g, w_in_bf, w_out_bf):
    n, d = x.shape
    d_ff = w_out_bf.shape[0]
    tm = _row_tile(n, 512)
    n_chunks = 2 if d_ff % (2 * LANES) == 0 else 1
    return pl.pallas_call(
        functools.partial(_ffn_body, d_ff=d_ff, n_chunks=n_chunks),
        grid=(n // tm,),
        in_specs=[pl.BlockSpec((tm, d), lambda i: (i, 0)),
                  _const_spec((1, d)), _const_spec((d, 2 * d_ff)), _const_spec((d_ff, d))],
        out_specs=pl.BlockSpec((tm, d), lambda i: (i, 0)),
        out_shape=jax.ShapeDtypeStruct((n, d), F32),
        compiler_params=pltpu.CompilerParams(dimension_semantics=("parallel",), vmem_limit_bytes=VMEM_LIMIT),
        name="ffn",
    )(x, g, w_in_bf, w_out_bf)


def _rope(x, cos, sin_signed, half):
    n = x.shape[-1]
    lane = lax.broadcasted_iota(I32, x.shape, 1)
    first = (lane % (2 * half)) < half
    partner = jnp.where(first, pltpu.roll(x, n - half, 1), pltpu.roll(x, half, 1))
    return x * cos + partner * sin_signed


def _head_rms(x, seg_ref, g):
    x2 = x * x
    hi = x2.astype(BF16)
    lo = (x2 - hi.astype(F32)).astype(BF16)
    ms = (jnp.dot(hi, seg_ref[...], preferred_element_type=F32)
          + jnp.dot(lo, seg_ref[...], preferred_element_type=F32))
    return x * lax.rsqrt(ms + EPS) * g


def _proj_body(x_ref, g_ref, w_ref, gq_ref, gk_ref, seg_ref, cqk_ref, sqk_ref, ci_ref, si_ref,
               u_ref, k_ref, v_ref, ki_ref, qb_ref, kb_ref, vb_ref, qib_ref, ki8b_ref, wi_ref, *, feature_major):
    h = _rms(x_ref[...], g_ref[...]).astype(BF16)
    p = jnp.dot(h, w_ref[...], preferred_element_type=F32)
    c0 = 0
    a_glu = p[:, c0:c0 + D_CONV]; c0 += D_CONV
    b_glu = p[:, c0:c0 + D_CONV]; c0 += D_CONV
    q = p[:, c0:c0 + D_ATTN]; c0 += D_ATTN
    k = p[:, c0:c0 + D_ATTN]; c0 += D_ATTN
    v = p[:, c0:c0 + D_ATTN]; c0 += D_ATTN
    qi = p[:, c0:c0 + D_IDX]; c0 += D_IDX
    ki8 = p[:, c0:c0 + D_IDX]; c0 += D_IDX
    wi = p[:, c0:c0 + LANES]

    u_ref[...] = a_glu * jax.nn.sigmoid(b_glu)
    q = _rope(_head_rms(q, seg_ref, gq_ref[...]), cqk_ref[...], sqk_ref[...], HEAD_DIM // 2)
    k = _rope(_head_rms(k, seg_ref, gk_ref[...]), cqk_ref[...], sqk_ref[...], HEAD_DIM // 2)
    qi = _rope(qi, ci_ref[...], si_ref[...], IDX_DIM // 2)
    ki8 = _rope(ki8, ci_ref[...], si_ref[...], IDX_DIM // 2)

    if feature_major:
        k_ref[0] = k.T
        v_ref[0] = v.T
        ki_ref[0] = ki8.T[:IDX_DIM, :]
    else:
        k_ref[...] = k
        v_ref[...] = v
        ki_ref[...] = ki8[:, :IDX_DIM]
    qb_ref[...] = (q * (HEAD_DIM ** -0.5)).astype(BF16)
    kb_ref[...] = k.astype(BF16)
    vb_ref[...] = v.astype(BF16)
    qib_ref[...] = qi.astype(BF16)
    ki8b_ref[...] = ki8.astype(BF16)
    wi_ref[...] = wi * IDX_W_SCALE


def _proj(x, g, w_bf, gq, gk, seg, tables, tab_rows, feature_major):
    n, d = x.shape
    tm = _row_tile(math.gcd(n, tab_rows), 256)
    tab_blocks = tab_rows // tm
    row = lambda w: pl.BlockSpec((tm, w), lambda i: (i, 0))
    tab = lambda w: pl.BlockSpec((tm, w), lambda i: (i % tab_blocks, 0))
    f32o = lambda w: jax.ShapeDtypeStruct((n, w), F32)
    bfo = lambda w: jax.ShapeDtypeStruct((n, w), BF16)
    if feature_major:
        cache = lambda w: pl.BlockSpec((1, w, tm), lambda i: (i // tab_blocks, 0, i % tab_blocks))
        cacheo = lambda w: jax.ShapeDtypeStruct((n // tab_rows, w, tab_rows), F32)
    else:
        cache, cacheo = row, f32o
    return pl.pallas_call(
        functools.partial(_proj_body, feature_major=feature_major),
        grid=(n // tm,),
        in_specs=[row(d), _const_spec((1, d)), _const_spec(w_bf.shape),
                  _const_spec((1, D_ATTN)), _const_spec((1, D_ATTN)), _const_spec((D_ATTN, D_ATTN)),
                  tab(D_ATTN), tab(D_ATTN), tab(D_IDX), tab(D_IDX)],
        out_specs=[row(D_CONV), cache(D_ATTN), cache(D_ATTN), cache(IDX_DIM),
                   row(D_ATTN), row(D_ATTN), row(D_ATTN), row(D_IDX), row(D_IDX), row(LANES)],
        out_shape=[f32o(D_CONV), cacheo(D_ATTN), cacheo(D_ATTN), cacheo(IDX_DIM),
                   bfo(D_ATTN), bfo(D_ATTN), bfo(D_ATTN), bfo(D_IDX), bfo(D_IDX), f32o(LANES)],
        compiler_params=pltpu.CompilerParams(dimension_semantics=("parallel",), vmem_limit_bytes=VMEM_LIMIT),
        name="proj",
    )(x, g, w_bf, gq, gk, seg, *tables)


def _conv_body(u_ref, uprev_ref, hist_ref, w_ref, b_ref, g_ref, bl_ref, o_ref, xs_ref, sh_ref, *, tc, n_t):
    xs_ref[HALO:HALO + tc, :] = u_ref[0]
    xs_ref[HALO + tc:HALO + tc + SUBLANES, :] = jnp.zeros((SUBLANES, D_CONV), F32)
    if n_t == 1:
        xs_ref[0:HALO, :] = hist_ref[0]
    else:
        i = pl.program_id(1)

        @pl.when(i == 0)
        def _():
            xs_ref[0:HALO, :] = hist_ref[0]

        @pl.when(i > 0)
        def _():
            xs_ref[0:HALO, :] = uprev_ref[0, tc - HALO:tc, :]

    n_sh = sh_ref.shape[1]
    for s in range(SUBLANES):
        sh_ref[s] = xs_ref[s:s + n_sh, :]

    rb = min(tc, 32)
    lead = HALO - (CONV_W - 1)
    for r0 in range(0, tc, rb):
        acc = jnp.broadcast_to(b_ref[...], (rb, D_CONV))
        for j in range(CONV_W):
            s = (lead + j) % SUBLANES
            base = r0 + lead + j - s
            acc = acc + w_ref[j:j + 1, :] * sh_ref[s, base:base + rb, :]
        mu = jnp.mean(acc, axis=-1, keepdims=True)
        yc = acc - mu
        y = yc * lax.rsqrt(jnp.mean(yc * yc, axis=-1, keepdims=True) + EPS) * g_ref[...] + bl_ref[...]
        o_ref[0, r0:r0 + rb, :] = y * jax.nn.sigmoid(y)


def _conv(u, hist, w_dw, b_dw, g_ln, b_ln):
    b, t, c = u.shape
    tc = _row_tile(t, 256)
    n_t = t // tc
    assert n_t == 1 or tc >= HALO
    vec = _const_spec((1, c))
    return pl.pallas_call(
        functools.partial(_conv_body, tc=tc, n_t=n_t),
        grid=(b, n_t),
        in_specs=[pl.BlockSpec((1, tc, c), lambda bi, i: (bi, i, 0)),
                  pl.BlockSpec((1, tc, c), lambda bi, i: (bi, jnp.maximum(i - 1, 0), 0)),
                  pl.BlockSpec((1, HALO, c), lambda bi, i: (bi, 0, 0)),
                  _const_spec((CONV_W, c)), vec, vec, vec],
        out_specs=pl.BlockSpec((1, tc, c), lambda bi, i: (bi, i, 0)),
        out_shape=jax.ShapeDtypeStruct((b, t, c), F32),
        scratch_shapes=[pltpu.VMEM((HALO + tc + SUBLANES, c), F32),
                        pltpu.VMEM((SUBLANES, HALO + tc, c), F32)],
        compiler_params=pltpu.CompilerParams(dimension_semantics=("parallel", "parallel")),
        name="conv",
    )(u, u, hist, w_dw, b_dw, g_ln, b_ln)


def _rank_to_f32(u):
    key = u + NEG_INF_KEY
    return lax.bitcast_convert_type(key ^ ((key >> 31) & 0x7FFFFFFF), F32)


def _count(mask):
    ones = jnp.where(mask, 1.0, 0.0)
    n = ones.shape[1]
    if n > COUNT_FOLD:
        parts = [ones[:, i:i + COUNT_FOLD] for i in range(0, n - n % COUNT_FOLD, COUNT_FOLD)]
        while len(parts) > 1:
            parts = [a + b for a, b in zip(parts[0::2], parts[1::2])] + parts[len(parts) & ~1:]
        total = jnp.sum(parts[0], axis=1, keepdims=True)
        if n % COUNT_FOLD:
            total = total + jnp.sum(ones[:, n - n % COUNT_FOLD:], axis=1, keepdims=True)
        return total
    return jnp.sum(ones, axis=1, keepdims=True)


def _select_topk_bias(sc_ref, code_ref, admissible, col, topk, n_idx_bits, bits_per_step):
    rows = sc_ref.shape[0]
    n_cand = 2 ** bits_per_step - 1
    assert 32 % bits_per_step == 0
    n_idx_bits = -(-n_idx_bits // bits_per_step) * bits_per_step

    def value_step(i, u):
        unit = lax.shift_left(jnp.int32(1), 32 - bits_per_step * (i + 1))
        sc = sc_ref[...]
        taken = jnp.zeros((rows, 1), I32)
        for j in range(1, n_cand + 1):
            taken = taken + jnp.where(_count(sc >= _rank_to_f32(u + j * unit)) >= topk, 1, 0)
        return u + taken * unit

    thr = _rank_to_f32(lax.fori_loop(0, 32 // bits_per_step, value_step, jnp.zeros((rows, 1), I32)))

    sc = sc_ref[...]
    code_ref[...] = jnp.where(sc > thr, -1, jnp.where(sc == thr, col, 2 ** n_idx_bits))

    def index_step(i, bound):
        unit = lax.shift_left(jnp.int32(1), n_idx_bits - bits_per_step * (i + 1))
        code = code_ref[...]
        taken = jnp.zeros((rows, 1), I32)
        for j in range(1, n_cand + 1):
            taken = taken + jnp.where(_count(code < bound + j * unit) < topk, 1, 0)
        return bound + taken * unit

    bound = lax.fori_loop(0, n_idx_bits // bits_per_step, index_step, jnp.zeros((rows, 1), I32))
    sc_ref[...] = jnp.where(admissible & (code_ref[...] <= bound), 0.0, MASKED_LOGIT)


def _pattern_to_f32(u, bf16_exact=False):
    key = jnp.maximum(u ^ INT_MIN, NEG_INF_KEY)
    bits = key ^ ((key >> 31) & 0x7FFFFFFF)
    if bf16_exact:
        bits = bits & -(1 << 16)
    return lax.bitcast_convert_type(bits, F32)


def _f32_to_pattern(x):
    bits = lax.bitcast_convert_type(x, I32)
    return (bits ^ ((bits >> 31) & 0x7FFFFFFF)) ^ INT_MIN


def _select_topk_bias_dense(sc_ref, sb_ref, tri_ref, topk):
    rows, n = sc_ref.shape
    one, zero = jnp.ones((), BF16), jnp.zeros((), BF16)
    sb_ref[...] = sc_ref[...].astype(BF16)

    def count_bf16(cand):
        cb = jnp.broadcast_to(cand, (rows, LANES)).astype(BF16)
        acc = jnp.where(sb_ref[:, 0:LANES] >= cb, one, zero)
        for j in range(1, n // LANES):
            acc = acc + jnp.where(sb_ref[:, j * LANES:(j + 1) * LANES] >= cb, one, zero)
        return jnp.sum(acc.astype(F32), axis=1, keepdims=True)

    def coarse_step(i, p):
        cand = p + lax.shift_left(jnp.int32(1), 15 - i)
        return jnp.where(count_bf16(_pattern_to_f32(cand << 16, bf16_exact=True)) >= topk, cand, p)

    p = lax.fori_loop(0, 16, coarse_step, jnp.zeros((rows, 1), I32))
    base = _f32_to_pattern(_pattern_to_f32(p << 16, bf16_exact=True)) - (1 << 15)

    def fine_step(i, d):
        cand = d + lax.shift_left(jnp.int32(1), 16 - i)
        return jnp.where(_count(sc_ref[...] >= _pattern_to_f32(base + cand)) >= topk, cand, d)

    thr = _pattern_to_f32(base + lax.fori_loop(0, 17, fine_step, jnp.zeros((rows, 1), I32)))

    need = jnp.where(thr == -jnp.inf, 0.0, topk - _count(sc_ref[...] > thr))
    cw = tri_ref.shape[0]
    seen = jnp.zeros((rows, 1), F32)
    for c in range(n // cw):
        s = sc_ref[:, c * cw:(c + 1) * cw]
        tie = jnp.where(s == thr, 1.0, 0.0)
        earlier = jnp.dot(tie.astype(BF16), tri_ref[...], preferred_element_type=F32) + seen
        keep_tie = jnp.where(earlier <= need, 0.0, MASKED_LOGIT)
        sc_ref[:, c * cw:(c + 1) * cw] = jnp.where(s > thr, 0.0, jnp.where(s == thr, keep_tie, MASKED_LOGIT))
        seen = seen + jnp.sum(tie, axis=1, keepdims=True)


def _attn_prompt_body(q_ref, k_ref, v_ref, qi_ref, ki8_ref, wi_ref, tri_ref, o_ref, bias_ref, sb_ref,
                      *, tq, q0, topk):
    t = q0 + tq
    qi = qi_ref[0]
    ki8 = ki8_ref[0]
    wi = wi_ref[0]
    idx_head = lax.broadcasted_iota(I32, (1, D_IDX), 1) // IDX_DIM
    scores = jnp.zeros((tq, t), F32)
    for h in range(N_IDX_HEADS):
        qh = qi * jnp.where(idx_head == h, 1.0, 0.0).astype(BF16)
        dots = lax.dot_general(qh, ki8, NT_DIMS, preferred_element_type=F32)
        scores = scores + wi[:, h:h + 1] * jnp.maximum(dots, 0.0)

    col = lax.broadcasted_iota(I32, (tq, t), 1)
    qpos = q0 + lax.broadcasted_iota(I32, (tq, t), 0)
    bias_ref[...] = jnp.where(col <= qpos, scores, -jnp.inf)
    _select_topk_bias_dense(bias_ref, sb_ref, tri_ref, topk)

    pair_head = lax.broadcasted_iota(I32, (1, LANES), 1) // HEAD_DIM
    for j in range(D_ATTN // LANES):
        sl = slice(j * LANES, (j + 1) * LANES)
        qj = q_ref[0, :, sl]
        kj = k_ref[0, :, sl]
        vj = v_ref[0, :, sl]
        out = jnp.zeros((tq, LANES), F32)
        for half in range(LANES // HEAD_DIM):
            mine = pair_head == half
            qh = qj * jnp.where(mine, 1.0, 0.0).astype(BF16)
            logits = lax.dot_general(qh, kj, NT_DIMS, preferred_element_type=F32) + bias_ref[...]
            m = jnp.max(logits, axis=1, keepdims=True)
            p = jnp.exp(logits - m)
            denom = jnp.sum(p, axis=1, keepdims=True)
            o = jnp.dot(p.astype(BF16), vj, preferred_element_type=F32) / denom
            out = jnp.where(mine, o, out)
        o_ref[0, :, sl] = out


def _attn_prompt(q_bf, k_bf, v_bf, qi_bf, ki8_bf, wi):
    b, t, _ = q_bf.shape
    tq = _row_tile(t, 256)
    topk = min(TOPK_MAX, t // 4)
    cw = min(tq, 2 * LANES)
    assert tq % cw == 0 and cw % LANES == 0
    tri = jnp.triu(jnp.ones((cw, cw), BF16))
    blocks = []
    for qb in range(t // tq):
        kv = (qb + 1) * tq
        qspec = lambda w, qb=qb: pl.BlockSpec((1, tq, w), lambda bi: (bi, qb, 0))
        kspec = lambda w, kv=kv: pl.BlockSpec((1, kv, w), lambda bi: (bi, 0, 0))
        blocks.append(pl.pallas_call(
            functools.partial(_attn_prompt_body, tq=tq, q0=qb * tq, topk=topk),
            grid=(b,),
            in_specs=[qspec(D_ATTN), kspec(D_ATTN), kspec(D_ATTN), qspec(D_IDX), kspec(D_IDX), qspec(LANES),
                      pl.BlockSpec((cw, cw), lambda bi: (0, 0))],
            out_specs=pl.BlockSpec((1, tq, D_ATTN), lambda bi: (bi, 0, 0)),
            out_shape=jax.ShapeDtypeStruct((b, tq, D_ATTN), F32),
            scratch_shapes=[pltpu.VMEM((tq, kv), F32), pltpu.VMEM((tq, kv), BF16)],
            compiler_params=pltpu.CompilerParams(dimension_semantics=("parallel",), vmem_limit_bytes=VMEM_LIMIT),
            name=f"attn_prompt_q{qb}",
        )(q_bf, k_bf, v_bf, qi_bf, ki8_bf, wi, tri))
    return jnp.concatenate(blocks, axis=1)


def _attn_sample_body(pt_ref, q_ref, knt_ref, vnt_ref, qi_ref, kint_ref, wrow_ref, ck_hbm, cv_hbm, cik_hbm,
                      o_ref, ikbuf, kbuf, vbuf, bias_ref, code_ref, sem_ik, sem_k, sem_v,
                      *, layer, n_seq, n_pages, ts, pc_ik, pc_kv, kv_slots, topk):
    seq = pl.program_id(0)
    n_past = n_pages * PAGE_SIZE
    s_pad = n_past + PAGE_SIZE
    rows = N_HEADS * ts
    n_ik_chunks = n_pages // pc_ik
    n_kv_chunks = n_pages // pc_kv
    n_kv_total = n_seq * n_kv_chunks

    def page_copy(hbm, buf, sem, s, page, slot, i):
        return pltpu.make_async_copy(hbm.at[layer, pt_ref[s, page]], buf.at[slot, i], sem.at[slot])

    def ik_copies(s):
        return [page_copy(cik_hbm, ikbuf, sem_ik, s, p, s % 2, p) for p in range(n_pages)]

    def kv_copies(g):
        s, c = g // n_kv_chunks, g % n_kv_chunks
        slot = g % kv_slots
        return [page_copy(hbm, buf, sem, s, c * pc_kv + i, slot, i)
                for hbm, buf, sem in ((ck_hbm, kbuf, sem_k), (cv_hbm, vbuf, sem_v)) for i in range(pc_kv)]

    @pl.when(seq == 0)
    def _():
        for cp in ik_copies(0):
            cp.start()
        for g in range(min(kv_slots - 1, n_kv_total)):
            for cp in kv_copies(g):
                cp.start()

    @pl.when(seq + 1 < n_seq)
    def _():
        for cp in ik_copies(seq + 1):
            cp.start()

    qi = qi_ref[0]
    wrow = wrow_ref[0]

    def page_scores(keys_t_bf):
        r = jnp.maximum(jnp.dot(qi, keys_t_bf, preferred_element_type=F32), 0.0) * wrow
        s = r[0:ts]
        for h in range(1, N_IDX_HEADS):
            s = s + r[h * ts:(h + 1) * ts]
        return s

    for cp in ik_copies(seq):
        cp.wait()

    def ik_chunk(c, carry):
        for i in range(pc_ik):
            page = c * pc_ik + i
            off = pl.multiple_of(page * PAGE_SIZE, PAGE_SIZE)
            bias_ref[:, pl.ds(off, PAGE_SIZE)] = page_scores(ikbuf[seq % 2, page].astype(BF16))
        return carry

    lax.fori_loop(0, n_ik_chunks, ik_chunk, 0)
    bias_ref[:, n_past:s_pad] = page_scores(kint_ref[0])

    col = lax.broadcasted_iota(I32, (ts, s_pad), 1)
    tpos = lax.broadcasted_iota(I32, (ts, s_pad), 0)
    admissible = (col - n_past) <= tpos
    bias_ref[...] = jnp.where(admissible, bias_ref[...], -jnp.inf)
    _select_topk_bias(bias_ref, code_ref, admissible, col, topk, max(1, (s_pad - 1).bit_length()), bits_per_step=2)

    qf = q_ref[0].astype(F32)
    row_head = lax.broadcasted_iota(I32, (rows, D_ATTN), 0) // ts
    lane_head = lax.broadcasted_iota(I32, (rows, D_ATTN), 1) // HEAD_DIM
    qbd = jnp.where(row_head == lane_head, jnp.concatenate([qf] * N_HEADS, axis=0), 0.0).astype(BF16)

    def attend(carry, pages_k, pages_v, bias):
        m, l, acc = carry
        n = len(pages_k) * PAGE_SIZE
        logits = jnp.concatenate([jnp.dot(qbd, kt, preferred_element_type=F32) for kt in pages_k], axis=1)
        logits = (logits.reshape(N_HEADS, ts, n) + bias[None]).reshape(rows, n)
        m_new = jnp.maximum(m, jnp.max(logits, axis=1, keepdims=True))
        alpha = jnp.exp(m - m_new)
        p = jnp.exp(logits - m_new)
        l = alpha * l + jnp.sum(p, axis=1, keepdims=True)
        p = p.astype(BF16)
        acc = alpha * acc
        for i, vt in enumerate(pages_v):
            acc = acc + lax.dot_general(p[:, i * PAGE_SIZE:(i + 1) * PAGE_SIZE], vt, NT_DIMS,
                                        preferred_element_type=F32)
        return m_new, l, acc

    n_chunk_keys = pc_kv * PAGE_SIZE

    def kv_chunk(c, carry):
        g = seq * n_kv_chunks + c
        slot = g % kv_slots

        @pl.when(g + kv_slots - 1 < n_kv_total)
        def _():
            for cp in kv_copies(g + kv_slots - 1):
                cp.start()

        for cp in kv_copies(g):
            cp.wait()
        off = pl.multiple_of(c * n_chunk_keys, n_chunk_keys)
        return attend(carry, [kbuf[slot, i].astype(BF16) for i in range(pc_kv)],
                      [vbuf[slot, i].astype(BF16) for i in range(pc_kv)], bias_ref[:, pl.ds(off, n_chunk_keys)])

    init = (jnp.full((rows, 1), SOFTMAX_FLOOR, F32), jnp.zeros((rows, 1), F32), jnp.zeros((rows, D_ATTN), F32))
    carry = lax.fori_loop(0, n_kv_chunks, kv_chunk, init)
    _, l, acc = attend(carry, [knt_ref[0]], [vnt_ref[0]], bias_ref[:, n_past:s_pad])

    acc = acc / l
    out = jnp.zeros((ts, D_ATTN), F32)
    head_of_lane = lax.broadcasted_iota(I32, (ts, D_ATTN), 1) // HEAD_DIM
    for h in range(N_HEADS):
        out = jnp.where(head_of_lane == h, acc[h * ts:(h + 1) * ts], out)
    o_ref[0] = out


def _attn_sample(layer, page_table, q_bf, knt_bf, vnt_bf, qi_rows, kint_bf, wrow, cache_kt, cache_vt, cache_ikt):
    db, ts, _ = q_bf.shape
    n_pages = page_table.shape[1]
    assert ts == SUBLANES, "one query block of 8 sublanes per sequence"
    pc_ik = _row_tile(n_pages, 32)
    pc_kv = _row_tile(n_pages, 16)
    kv_slots = 4
    s_pad = (n_pages + 1) * PAGE_SIZE
    topk = min(TOPK_MAX, (n_pages * PAGE_SIZE + ts) // 4)
    rows = N_HEADS * ts
    per_seq = lambda r, w: pl.BlockSpec((1, r, w), lambda s, pt: (s, 0, 0))
    any_spec = pl.BlockSpec(memory_space=pl.ANY)
    grid_spec = pltpu.PrefetchScalarGridSpec(
        num_scalar_prefetch=1,
        grid=(db,),
        in_specs=[per_seq(ts, D_ATTN), per_seq(D_ATTN, PAGE_SIZE), per_seq(D_ATTN, PAGE_SIZE),
                  per_seq(rows, IDX_DIM), per_seq(IDX_DIM, PAGE_SIZE), per_seq(rows, 1),
                  any_spec, any_spec, any_spec],
        out_specs=per_seq(ts, D_ATTN),
        scratch_shapes=[pltpu.VMEM((2, n_pages, IDX_DIM, PAGE_SIZE), F32),
                        pltpu.VMEM((kv_slots, pc_kv, D_ATTN, PAGE_SIZE), F32),
                        pltpu.VMEM((kv_slots, pc_kv, D_ATTN, PAGE_SIZE), F32),
                        pltpu.VMEM((ts, s_pad), F32), pltpu.VMEM((ts, s_pad), I32),
                        pltpu.SemaphoreType.DMA((2,)), pltpu.SemaphoreType.DMA((kv_slots,)),
                        pltpu.SemaphoreType.DMA((kv_slots,))])
    return pl.pallas_call(
        functools.partial(_attn_sample_body, layer=layer, n_seq=db, n_pages=n_pages, ts=ts, pc_ik=pc_ik,
                          pc_kv=pc_kv, kv_slots=kv_slots, topk=topk),
        grid_spec=grid_spec,
        out_shape=jax.ShapeDtypeStruct((db, ts, D_ATTN), F32),
        compiler_params=pltpu.CompilerParams(dimension_semantics=("arbitrary",), vmem_limit_bytes=VMEM_LIMIT),
        name="attn_sample",
    )(page_table, q_bf, knt_bf, vnt_bf, qi_rows, kint_bf, wrow, cache_kt, cache_vt, cache_ikt)


def _merge_body(x_ref, cy_ref, att_ref, g_ref, wg_ref, wc_ref, wa_ref, wo_ref, o_ref, *, d):
    x = x_ref[...]
    h = _rms(x, g_ref[...]).astype(BF16)
    gates = jax.nn.sigmoid(jnp.dot(h, wg_ref[...], preferred_element_type=F32))
    conv_out = jnp.dot(cy_ref[...].astype(BF16), wc_ref[...], preferred_element_type=F32)
    a = jnp.dot(att_ref[...].astype(BF16), wa_ref[...], preferred_element_type=F32)
    mixed = (gates[:, :d] * conv_out + gates[:, d:] * a).astype(BF16)
    o_ref[...] = x + jnp.dot(mixed, wo_ref[...], preferred_element_type=F32)


def _merge(x, conv_y, att, g, wg_bf, wc_bf, wa_bf, wo_bf):
    n, d = x.shape
    tm = _row_tile(n, 512)
    row = lambda w: pl.BlockSpec((tm, w), lambda i: (i, 0))
    return pl.pallas_call(
        functools.partial(_merge_body, d=d),
        grid=(n // tm,),
        in_specs=[row(d), row(D_CONV), row(D_ATTN), _const_spec((1, d)), _const_spec(wg_bf.shape),
                  _const_spec(wc_bf.shape), _const_spec(wa_bf.shape), _const_spec(wo_bf.shape)],
        out_specs=row(d),
        out_shape=jax.ShapeDtypeStruct((n, d), F32),
        compiler_params=pltpu.CompilerParams(dimension_semantics=("parallel",), vmem_limit_bytes=VMEM_LIMIT),
        name="merge",
    )(x, conv_y, att, g, wg_bf, wc_bf, wa_bf, wo_bf)


def _rope_tables(pos, head_dim, n_heads):
    half = head_dim // 2
    inv = jnp.float32(ROPE_THETA) ** (-jnp.arange(half, dtype=F32) * (2.0 / head_dim))
    ang = pos.astype(F32)[:, None] * inv[None, :]
    cos = jnp.tile(jnp.concatenate([jnp.cos(ang), jnp.cos(ang)], axis=1), (1, n_heads))
    sin = jnp.tile(jnp.concatenate([-jnp.sin(ang), jnp.sin(ang)], axis=1), (1, n_heads))
    return cos, sin


def kernel(x_prompt, x_sample, cache_k, cache_v, cache_idx_k, state_conv, page_table, g_ffn1, w_ffn1_in, w_ffn1_out, g_mix, w_in, g_q, g_k, w_dw, b_dw, g_cln, b_cln, w_cout, w_aout, w_o, g_ffn2, w_ffn2_in, w_ffn2_out):
    b, t, d = x_prompt.shape
    db, ts, _ = x_sample.shape
    depth = g_mix.shape[0]
    n_pages = page_table.shape[1]
    n_phys = cache_k.shape[1]
    past_len = n_pages * PAGE_SIZE
    hist_rows = CONV_W - 1

    pos_p = jnp.arange(t)
    pos_s = jnp.tile(past_len + jnp.arange(ts), db)
    tab_p = _rope_tables(pos_p, HEAD_DIM, N_HEADS) + _rope_tables(pos_p, IDX_DIM, N_IDX_HEADS)
    tab_s = _rope_tables(pos_s, HEAD_DIM, N_HEADS) + _rope_tables(pos_s, IDX_DIM, N_IDX_HEADS)
    seg = jnp.kron(jnp.eye(N_HEADS, dtype=F32), jnp.full((HEAD_DIM, HEAD_DIM), 1.0 / HEAD_DIM, F32)).astype(BF16)

    xp = x_prompt.reshape(b * t, d)
    xs = x_sample.reshape(db * ts, d)
    outs = [[] for _ in range(8)]
    for l in range(depth):
        vec = lambda a: a[l][None, :]
        w1i, w1o = w_ffn1_in[l].astype(BF16), w_ffn1_out[l].astype(BF16)
        w2i, w2o = w_ffn2_in[l].astype(BF16), w_ffn2_out[l].astype(BF16)
        c_main = 2 * D_CONV + 3 * D_ATTN + D_IDX
        w_ki = w_in[l][:, c_main:c_main + IDX_DIM]
        w_wi = w_in[l][:, c_main + IDX_DIM:c_main + IDX_DIM + N_IDX_HEADS]
        w_proj = jnp.concatenate([w_in[l][:, :c_main], jnp.tile(w_ki, (1, N_IDX_HEADS)),
                                  jnp.pad(w_wi, ((0, 0), (0, LANES - N_IDX_HEADS)))], axis=1).astype(BF16)
        w_gates = w_in[l][:, c_main + IDX_DIM + N_IDX_HEADS:].astype(BF16)
        gq = jnp.tile(g_q[l], N_HEADS)[None, :]
        gk = jnp.tile(g_k[l], N_HEADS)[None, :]
        wc, wa, wo = w_cout[l].astype(BF16), w_aout[l].astype(BF16), w_o[l].astype(BF16)

        xp = _ffn(xp, vec(g_ffn1), w1i, w1o)
        xs = _ffn(xs, vec(g_ffn1), w1i, w1o)
        up, kp, vp, kip, qbp, kbp, vbp, qibp, ki8bp, wip = _proj(xp, vec(g_mix), w_proj, gq, gk, seg, tab_p, t, True)
        us, ks, vs, kis, qbs, kbs, vbs, qibs, ki8bs, wis = _proj(xs, vec(g_mix), w_proj, gq, gk, seg, tab_s, db * ts,
                                                                 False)

        up3, us3 = up.reshape(b, t, D_CONV), us.reshape(db, ts, D_CONV)
        lead = ((0, 0), (HALO - hist_rows, 0), (0, 0))
        hist_p = jnp.zeros((b, HALO, D_CONV), F32)
        hist_s = jnp.pad(state_conv[l], lead)
        conv_args = (w_dw[l], vec(b_dw), vec(g_cln), vec(b_cln))
        cyp = _conv(up3, hist_p, *conv_args)
        cys = _conv(us3, hist_s, *conv_args)
        st_p = jnp.concatenate([jnp.zeros((b, hist_rows, D_CONV), F32), up3], axis=1)[:, -hist_rows:]
        st_s = jnp.concatenate([state_conv[l], us3], axis=1)[:, -hist_rows:]

        r3 = lambda a, n, w: a.reshape(n, -1, w)
        att_p = _attn_prompt(r3(qbp, b, D_ATTN), r3(kbp, b, D_ATTN), r3(vbp, b, D_ATTN),
                             r3(qibp, b, D_IDX), r3(ki8bp, b, D_IDX), r3(wip, b, LANES))
        head_major = lambda a, nh, w: a.reshape(db, ts, nh, w).transpose(0, 2, 1, 3).reshape(db, nh * ts, w)
        new_page_t = lambda a: jnp.pad(a.reshape(db, ts, -1).transpose(0, 2, 1), ((0, 0), (0, 0), (0, PAGE_SIZE - ts)))
        att_s = _attn_sample(
            l, page_table, r3(qbs, db, D_ATTN), new_page_t(kbs), new_page_t(vbs),
            head_major(qibs, N_IDX_HEADS, IDX_DIM), new_page_t(ki8bs[:, :IDX_DIM]),
            head_major(wis[:, :N_IDX_HEADS], N_IDX_HEADS, 1),
            cache_k.transpose(0, 1, 3, 4, 2).reshape(depth, n_phys, D_ATTN, PAGE_SIZE),
            cache_v.transpose(0, 1, 3, 4, 2).reshape(depth, n_phys, D_ATTN, PAGE_SIZE),
            cache_idx_k.transpose(0, 1, 3, 2))

        xp = _merge(xp, cyp.reshape(b * t, D_CONV), att_p.reshape(b * t, D_ATTN), vec(g_mix), w_gates, wc, wa, wo)
        xs = _merge(xs, cys.reshape(db * ts, D_CONV), att_s.reshape(db * ts, D_ATTN), vec(g_mix), w_gates, wc, wa, wo)
        xp = _ffn(xp, vec(g_ffn2), w2i, w2o)
        xs = _ffn(xs, vec(g_ffn2), w2i, w2o)

        token_major = lambda a: a.reshape(b, N_HEADS, HEAD_DIM, t).transpose(0, 3, 1, 2)
        for lst, val in zip(outs, (token_major(kp), token_major(vp), kip.transpose(0, 2, 1), st_p,
                                   ks.reshape(db, ts, N_HEADS, HEAD_DIM), vs.reshape(db, ts, N_HEADS, HEAD_DIM),
                                   kis.reshape(db, ts, IDX_DIM), st_s)):
            lst.append(val)

    return (xp.reshape(b, t, d), xs.reshape(db, ts, d)) + tuple(jnp.stack(o) for o in outs)
```

```python
import functools
import math

import jax
import jax.numpy as jnp
from jax import lax
from jax.experimental import pallas as pl
from jax.experimental.pallas import tpu as pltpu

N_HEADS = 8
HEAD_DIM = 64
D_ATTN = N_HEADS * HEAD_DIM
D_CONV = 512
CONV_W = 31
N_IDX_HEADS = 8
IDX_DIM = 32
D_IDX = N_IDX_HEADS * IDX_DIM
TOPK_MAX = 256
PAGE_SIZE = 128
ROPE_THETA = 10000.0
EPS = 1e-6
IDX_W_SCALE = D_IDX ** -0.5

LANES = 128
SUBLANES = 8
HALO = 32
INT_MIN = -2 ** 31
NEG_INF_KEY = INT_MIN + 0x7FFFFF
MASKED_LOGIT = -1e30
SOFTMAX_FLOOR = 0.25 * MASKED_LOGIT
COUNT_FOLD = 16 * LANES
VMEM_LIMIT = 56 * 1024 * 1024

F32 = jnp.float32
BF16 = jnp.bfloat16
I32 = jnp.int32
NT_DIMS = (((1,), (1,)), ((), ()))


def _row_tile(n, cap):
    t = cap
    while n % t:
        t //= 2
    return t


def _rms(x, g):
    return x * lax.rsqrt(jnp.mean(x * x, axis=-1, keepdims=True) + EPS) * g


def _const_spec(shape):
    nd = len(shape)
    return pl.BlockSpec(shape, lambda *_: (0,) * nd, pipeline_mode=pl.Buffered(1))


def _ffn_body(x_ref, g_ref, win_ref, wout_ref, o_ref, *, d_ff, n_chunks):
    x = x_ref[...]
    h = _rms(x, g_ref[...]).astype(BF16)
    fc = d_ff // n_chunks
    acc = jnp.zeros_like(x)
    for c in range(n_chunks):
        gate = jnp.dot(h, win_ref[:, c * fc:(c + 1) * fc], preferred_element_type=F32)
        up = jnp.dot(h, win_ref[:, d_ff + c * fc:d_ff + (c + 1) * fc], preferred_element_type=F32)
        act = (gate * jax.nn.sigmoid(gate) * up).astype(BF16)
        acc = acc + jnp.dot(act, wout_ref[c * fc:(c + 1) * fc, :], preferred_element_type=F32)
    o_ref[...] = x + 0.5 * acc


def _ffn(x, g, w_in_bf, w_out_bf):
    n, d = x.shape
    d_ff = w_out_bf.shape[0]
    tm = _row_tile(n, 512)
    n_chunks = 2 if d_ff % (2 * LANES) == 0 else 1
    return pl.pallas_call(
        functools.partial(_ffn_body, d_ff=d_ff, n_chunks=n_chunks),
        grid=(n // tm,),
        in_specs=[pl.BlockSpec((tm, d), lambda i: (i, 0)),
                  _const_spec((1, d)), _const_spec((d, 2 * d_ff)), _const_spec((d_ff, d))],
        out_specs=pl.BlockSpec((tm, d), lambda i: (i, 0)),
        out_shape=jax.ShapeDtypeStruct((n, d), F32),
        compiler_params=pltpu.CompilerParams(dimension_semantics=("parallel",), vmem_limit_bytes=VMEM_LIMIT),
        name="ffn",
    )(x, g, w_in_bf, w_out_bf)


def _rope(x, cos, sin_signed, half):
    n = x.shape[-1]
    lane = lax.broadcasted_iota(I32, x.shape, 1)
    first = (lane % (2 * half)) < half
    partner = jnp.where(first, pltpu.roll(x, n - half, 1), pltpu.roll(x, half, 1))
    return x * cos + partner * sin_signed


def _head_rms(x, seg_ref, g):
    x2 = x * x
    hi = x2.astype(BF16)
    lo = (x2 - hi.astype(F32)).astype(BF16)
    ms = (jnp.dot(hi, seg_ref[...], preferred_element_type=F32)
          + jnp.dot(lo, seg_ref[...], preferred_element_type=F32))
    return x * lax.rsqrt(ms + EPS) * g


def _proj_body(x_ref, g_ref, w_ref, gq_ref, gk_ref, seg_ref, cqk_ref, sqk_ref, ci_ref, si_ref,
               u_ref, k_ref, v_ref, ki_ref, qb_ref, kb_ref, vb_ref, qib_ref, ki8b_ref, wi_ref, *, feature_major):
    h = _rms(x_ref[...], g_ref[...]).astype(BF16)
    p = jnp.dot(h, w_ref[...], preferred_element_type=F32)
    c0 = 0
    a_glu = p[:, c0:c0 + D_CONV]; c0 += D_CONV
    b_glu = p[:, c0:c0 + D_CONV]; c0 += D_CONV
    q = p[:, c0:c0 + D_ATTN]; c0 += D_ATTN
    k = p[:, c0:c0 + D_ATTN]; c0 += D_ATTN
    v = p[:, c0:c0 + D_ATTN]; c0 += D_ATTN
    qi = p[:, c0:c0 + D_IDX]; c0 += D_IDX
    ki8 = p[:, c0:c0 + D_IDX]; c0 += D_IDX
    wi = p[:, c0:c0 + LANES]

    u_ref[...] = a_glu * jax.nn.sigmoid(b_glu)
    q = _rope(_head_rms(q, seg_ref, gq_ref[...]), cqk_ref[...], sqk_ref[...], HEAD_DIM // 2)
    k = _rope(_head_rms(k, seg_ref, gk_ref[...]), cqk_ref[...], sqk_ref[...], HEAD_DIM // 2)
    qi = _rope(qi, ci_ref[...], si_ref[...], IDX_DIM // 2)
    ki8 = _rope(ki8, ci_ref[...], si_ref[...], IDX_DIM // 2)

    if feature_major:
        k_ref[0] = k.T
        v_ref[0] = v.T
        ki_ref[0] = ki8.T[:IDX_DIM, :]
    else:
        k_ref[...] = k
        v_ref[...] = v
        ki_ref[...] = ki8[:, :IDX_DIM]
    qb_ref[...] = (q * (HEAD_DIM ** -0.5)).astype(BF16)
    kb_ref[...] = k.astype(BF16)
    vb_ref[...] = v.astype(BF16)
    qib_ref[...] = qi.astype(BF16)
    ki8b_ref[...] = ki8.astype(BF16)
    wi_ref[...] = wi * IDX_W_SCALE


def _proj(x, g, w_bf, gq, gk, seg, tables, tab_rows, feature_major):
    n, d = x.shape
    tm = _row_tile(math.gcd(n, tab_rows), 256)
    tab_blocks = tab_rows // tm
    row = lambda w: pl.BlockSpec((tm, w), lambda i: (i, 0))
    tab = lambda w: pl.BlockSpec((tm, w), lambda i: (i % tab_blocks, 0))
    f32o = lambda w: jax.ShapeDtypeStruct((n, w), F32)
    bfo = lambda w: jax.ShapeDtypeStruct((n, w), BF16)
    if feature_major:
        cache = lambda w: pl.BlockSpec((1, w, tm), lambda i: (i // tab_blocks, 0, i % tab_blocks))
        cacheo = lambda w: jax.ShapeDtypeStruct((n // tab_rows, w, tab_rows), F32)
    else:
        cache, cacheo = row, f32o
    return pl.pallas_call(
        functools.partial(_proj_body, feature_major=feature_major),
        grid=(n // tm,),
        in_specs=[row(d), _const_spec((1, d)), _const_spec(w_bf.shape),
                  _const_spec((1, D_ATTN)), _const_spec((1, D_ATTN)), _const_spec((D_ATTN, D_ATTN)),
                  tab(D_ATTN), tab(D_ATTN), tab(D_IDX), tab(D_IDX)],
        out_specs=[row(D_CONV), cache(D_ATTN), cache(D_ATTN), cache(IDX_DIM),
                   row(D_ATTN), row(D_ATTN), row(D_ATTN), row(D_IDX), row(D_IDX), row(LANES)],
        out_shape=[f32o(D_CONV), cacheo(D_ATTN), cacheo(D_ATTN), cacheo(IDX_DIM),
                   bfo(D_ATTN), bfo(D_ATTN), bfo(D_ATTN), bfo(D_IDX), bfo(D_IDX), f32o(LANES)],
        compiler_params=pltpu.CompilerParams(dimension_semantics=("parallel",), vmem_limit_bytes=VMEM_LIMIT),
        name="proj",
    )(x, g, w_bf, gq, gk, seg, *tables)


def _conv_body(u_ref, uprev_ref, hist_ref, w_ref, b_ref, g_ref, bl_ref, o_ref, xs_ref, sh_ref, *, tc, n_t):
    xs_ref[HALO:HALO + tc, :] = u_ref[0]
    xs_ref[HALO + tc:HALO + tc + SUBLANES, :] = jnp.zeros((SUBLANES, D_CONV), F32)
    if n_t == 1:
        xs_ref[0:HALO, :] = hist_ref[0]
    else:
        i = pl.program_id(1)

        @pl.when(i == 0)
        def _():
            xs_ref[0:HALO, :] = hist_ref[0]

        @pl.when(i > 0)
        def _():
            xs_ref[0:HALO, :] = uprev_ref[0, tc - HALO:tc, :]

    n_sh = sh_ref.shape[1]
    for s in range(SUBLANES):
        sh_ref[s] = xs_ref[s:s + n_sh, :]

    rb = min(tc, 32)
    lead = HALO - (CONV_W - 1)
    for r0 in range(0, tc, rb):
        acc = jnp.broadcast_to(b_ref[...], (rb, D_CONV))
        for j in range(CONV_W):
            s = (lead + j) % SUBLANES
            base = r0 + lead + j - s
            acc = acc + w_ref[j:j + 1, :] * sh_ref[s, base:base + rb, :]
        mu = jnp.mean(acc, axis=-1, keepdims=True)
        yc = acc - mu
        y = yc * lax.rsqrt(jnp.mean(yc * yc, axis=-1, keepdims=True) + EPS) * g_ref[...] + bl_ref[...]
        o_ref[0, r0:r0 + rb, :] = y * jax.nn.sigmoid(y)


def _conv(u, hist, w_dw, b_dw, g_ln, b_ln):
    b, t, c = u.shape
    tc = _row_tile(t, 256)
    n_t = t // tc
    assert n_t == 1 or tc >= HALO
    vec = _const_spec((1, c))
    return pl.pallas_call(
        functools.partial(_conv_body, tc=tc, n_t=n_t),
        grid=(b, n_t),
        in_specs=[pl.BlockSpec((1, tc, c), lambda bi, i: (bi, i, 0)),
                  pl.BlockSpec((1, tc, c), lambda bi, i: (bi, jnp.maximum(i - 1, 0), 0)),
                  pl.BlockSpec((1, HALO, c), lambda bi, i: (bi, 0, 0)),
                  _const_spec((CONV_W, c)), vec, vec, vec],
        out_specs=pl.BlockSpec((1, tc, c), lambda bi, i: (bi, i, 0)),
        out_shape=jax.ShapeDtypeStruct((b, t, c), F32),
        scratch_shapes=[pltpu.VMEM((HALO + tc + SUBLANES, c), F32),
                        pltpu.VMEM((SUBLANES, HALO + tc, c), F32)],
        compiler_params=pltpu.CompilerParams(dimension_semantics=("parallel", "parallel")),
        name="conv",
    )(u, u, hist, w_dw, b_dw, g_ln, b_ln)


def _rank_to_f32(u):
    key = u + NEG_INF_KEY
    return lax.bitcast_convert_type(key ^ ((key >> 31) & 0x7FFFFFFF), F32)


def _count(mask):
    ones = jnp.where(mask, 1.0, 0.0)
    n = ones.shape[1]
    if n > COUNT_FOLD:
        parts = [ones[:, i:i + COUNT_FOLD] for i in range(0, n - n % COUNT_FOLD, COUNT_FOLD)]
        while len(parts) > 1:
            parts = [a + b for a, b in zip(parts[0::2], parts[1::2])] + parts[len(parts) & ~1:]
        total = jnp.sum(parts[0], axis=1, keepdims=True)
        if n % COUNT_FOLD:
            total = total + jnp.sum(ones[:, n - n % COUNT_FOLD:], axis=1, keepdims=True)
        return total
    return jnp.sum(ones, axis=1, keepdims=True)


def _select_topk_bias(sc_ref, code_ref, admissible, col, topk, n_idx_bits, bits_per_step):
    rows = sc_ref.shape[0]
    n_cand = 2 ** bits_per_step - 1
    assert 32 % bits_per_step == 0
    n_idx_bits = -(-n_idx_bits // bits_per_step) * bits_per_step

    def value_step(i, u):
        unit = lax.shift_left(jnp.int32(1), 32 - bits_per_step * (i + 1))
        sc = sc_ref[...]
        taken = jnp.zeros((rows, 1), I32)
        for j in range(1, n_cand + 1):
            taken = taken + jnp.where(_count(sc >= _rank_to_f32(u + j * unit)) >= topk, 1, 0)
        return u + taken * unit

    thr = _rank_to_f32(lax.fori_loop(0, 32 // bits_per_step, value_step, jnp.zeros((rows, 1), I32)))

    sc = sc_ref[...]
    code_ref[...] = jnp.where(sc > thr, -1, jnp.where(sc == thr, col, 2 ** n_idx_bits))

    def index_step(i, bound):
        unit = lax.shift_left(jnp.int32(1), n_idx_bits - bits_per_step * (i + 1))
        code = code_ref[...]
        taken = jnp.zeros((rows, 1), I32)
        for j in range(1, n_cand + 1):
            taken = taken + jnp.where(_count(code < bound + j * unit) < topk, 1, 0)
        return bound + taken * unit

    bound = lax.fori_loop(0, n_idx_bits // bits_per_step, index_step, jnp.zeros((rows, 1), I32))
    sc_ref[...] = jnp.where(admissible & (code_ref[...] <= bound), 0.0, MASKED_LOGIT)


def _pattern_to_f32(u, bf16_exact=False):
    key = jnp.maximum(u ^ INT_MIN, NEG_INF_KEY)
    bits = key ^ ((key >> 31) & 0x7FFFFFFF)
    if bf16_exact:
        bits = bits & -(1 << 16)
    return lax.bitcast_convert_type(bits, F32)


def _f32_to_pattern(x):
    bits = lax.bitcast_convert_type(x, I32)
    return (bits ^ ((bits >> 31) & 0x7FFFFFFF)) ^ INT_MIN


def _select_topk_bias_dense(sc_ref, sb_ref, tri_ref, topk):
    rows, n = sc_ref.shape

    def unrolled(n_steps, body, carry):
        for i in range(n_steps):
            carry = body(i, carry)
        return carry
    one, zero = jnp.ones((), BF16), jnp.zeros((), BF16)
    sb_ref[...] = sc_ref[...].astype(BF16)

    def count_bf16(cand):
        cb = jnp.broadcast_to(cand, (rows, LANES)).astype(BF16)
        acc = jnp.where(sb_ref[:, 0:LANES] >= cb, one, zero)
        for j in range(1, n // LANES):
            acc = acc + jnp.where(sb_ref[:, j * LANES:(j + 1) * LANES] >= cb, one, zero)
        return jnp.sum(acc.astype(F32), axis=1, keepdims=True)

    def coarse_step(i, p):
        cand = p + (1 << (15 - i))
        return jnp.where(count_bf16(_pattern_to_f32(cand << 16, bf16_exact=True)) >= topk, cand, p)

    p = unrolled(16, coarse_step, jnp.zeros((rows, 1), I32))
    base = _f32_to_pattern(_pattern_to_f32(p << 16, bf16_exact=True)) - (1 << 15)

    def fine_step(i, d):
        cand = d + (1 << (16 - i))
        return jnp.where(_count(sc_ref[...] >= _pattern_to_f32(base + cand)) >= topk, cand, d)

    thr = _pattern_to_f32(base + unrolled(17, fine_step, jnp.zeros((rows, 1), I32)))

    need = jnp.where(thr == -jnp.inf, 0.0, topk - _count(sc_ref[...] > thr))
    cw = tri_ref.shape[0]
    seen = jnp.zeros((rows, 1), F32)
    for c in range(n // cw):
        s = sc_ref[:, c * cw:(c + 1) * cw]
        tie = jnp.where(s == thr, 1.0, 0.0)
        earlier = jnp.dot(tie.astype(BF16), tri_ref[...], preferred_element_type=F32) + seen
        keep_tie = jnp.where(earlier <= need, 0.0, MASKED_LOGIT)
        sc_ref[:, c * cw:(c + 1) * cw] = jnp.where(s > thr, 0.0, jnp.where(s == thr, keep_tie, MASKED_LOGIT))
        seen = seen + jnp.sum(tie, axis=1, keepdims=True)


def _attn_prompt_body(q_ref, k_ref, v_ref, qi_ref, ki8_ref, wi_ref, tri_ref, o_ref, bias_ref, sb_ref,
                      *, tq, q0, topk):
    t = q0 + tq
    qi = qi_ref[0]
    ki8 = ki8_ref[0]
    wi = wi_ref[0]
    idx_head = lax.broadcasted_iota(I32, (1, D_IDX), 1) // IDX_DIM
    scores = jnp.zeros((tq, t), F32)
    for h in range(N_IDX_HEADS):
        qh = qi * jnp.where(idx_head == h, 1.0, 0.0).astype(BF16)
        dots = lax.dot_general(qh, ki8, NT_DIMS, preferred_element_type=F32)
        scores = scores + wi[:, h:h + 1] * jnp.maximum(dots, 0.0)

    col = lax.broadcasted_iota(I32, (tq, t), 1)
    qpos = q0 + lax.broadcasted_iota(I32, (tq, t), 0)
    bias_ref[...] = jnp.where(col <= qpos, scores, -jnp.inf)
    _select_topk_bias_dense(bias_ref, sb_ref, tri_ref, topk)

    pair_head = lax.broadcasted_iota(I32, (1, LANES), 1) // HEAD_DIM
    for j in range(D_ATTN // LANES):
        sl = slice(j * LANES, (j + 1) * LANES)
        qj = q_ref[0, :, sl]
        kj = k_ref[0, :, sl]
        vj = v_ref[0, :, sl]
        out = jnp.zeros((tq, LANES), F32)
        for half in range(LANES // HEAD_DIM):
            mine = pair_head == half
            qh = qj * jnp.where(mine, 1.0, 0.0).astype(BF16)
            logits = lax.dot_general(qh, kj, NT_DIMS, preferred_element_type=F32) + bias_ref[...]
            m = jnp.max(logits, axis=1, keepdims=True)
            p = jnp.exp(logits - m)
            denom = jnp.sum(p, axis=1, keepdims=True)
            o = jnp.dot(p.astype(BF16), vj, preferred_element_type=F32) / denom
            out = jnp.where(mine, o, out)
        o_ref[0, :, sl] = out


def _attn_prompt(q_bf, k_bf, v_bf, qi_bf, ki8_bf, wi):
    b, t, _ = q_bf.shape
    tq = _row_tile(t, 256)
    topk = min(TOPK_MAX, t // 4)
    cw = min(tq, 2 * LANES)
    assert tq % cw == 0 and cw % LANES == 0
    tri = jnp.triu(jnp.ones((cw, cw), BF16))
    blocks = []
    for qb in range(t // tq):
        kv = (qb + 1) * tq
        qspec = lambda w, qb=qb: pl.BlockSpec((1, tq, w), lambda bi: (bi, qb, 0))
        kspec = lambda w, kv=kv: pl.BlockSpec((1, kv, w), lambda bi: (bi, 0, 0))
        blocks.append(pl.pallas_call(
            functools.partial(_attn_prompt_body, tq=tq, q0=qb * tq, topk=topk),
            grid=(b,),
            in_specs=[qspec(D_ATTN), kspec(D_ATTN), kspec(D_ATTN), qspec(D_IDX), kspec(D_IDX), qspec(LANES),
                      pl.BlockSpec((cw, cw), lambda bi: (0, 0))],
            out_specs=pl.BlockSpec((1, tq, D_ATTN), lambda bi: (bi, 0, 0)),
            out_shape=jax.ShapeDtypeStruct((b, tq, D_ATTN), F32),
            scratch_shapes=[pltpu.VMEM((tq, kv), F32), pltpu.VMEM((tq, kv), BF16)],
            compiler_params=pltpu.CompilerParams(dimension_semantics=("parallel",), vmem_limit_bytes=VMEM_LIMIT),
            name=f"attn_prompt_q{qb}",
        )(q_bf, k_bf, v_bf, qi_bf, ki8_bf, wi, tri))
    return jnp.concatenate(blocks, axis=1)


def _attn_sample_body(pt_ref, q_ref, knt_ref, vnt_ref, qi_ref, kint_ref, wrow_ref, ck_hbm, cv_hbm, cik_hbm,
                      o_ref, ikbuf, kbuf, vbuf, bias_ref, code_ref, sem_ik, sem_k, sem_v,
                      *, layer, n_seq, n_pages, ts, pc_ik, pc_kv, kv_slots, topk):
    seq = pl.program_id(0)
    n_past = n_pages * PAGE_SIZE
    s_pad = n_past + PAGE_SIZE
    rows = N_HEADS * ts
    n_ik_chunks = n_pages // pc_ik
    n_kv_chunks = n_pages // pc_kv
    n_kv_total = n_seq * n_kv_chunks

    def page_copy(hbm, buf, sem, s, page, slot, i):
        return pltpu.make_async_copy(hbm.at[layer, pt_ref[s, page]], buf.at[slot, i], sem.at[slot])

    def ik_copies(s):
        return [page_copy(cik_hbm, ikbuf, sem_ik, s, p, s % 2, p) for p in range(n_pages)]

    def kv_copies(g):
        s, c = g // n_kv_chunks, g % n_kv_chunks
        slot = g % kv_slots
        return [page_copy(hbm, buf, sem, s, c * pc_kv + i, slot, i)
                for hbm, buf, sem in ((ck_hbm, kbuf, sem_k), (cv_hbm, vbuf, sem_v)) for i in range(pc_kv)]

    @pl.when(seq == 0)
    def _():
        for cp in ik_copies(0):
            cp.start()
        for g in range(min(kv_slots - 1, n_kv_total)):
            for cp in kv_copies(g):
                cp.start()

    @pl.when(seq + 1 < n_seq)
    def _():
        for cp in ik_copies(seq + 1):
            cp.start()

    qi = qi_ref[0]
    wrow = wrow_ref[0]

    def page_scores(keys_t_bf):
        r = jnp.maximum(jnp.dot(qi, keys_t_bf, preferred_element_type=F32), 0.0) * wrow
        s = r[0:ts]
        for h in range(1, N_IDX_HEADS):
            s = s + r[h * ts:(h + 1) * ts]
        return s

    for cp in ik_copies(seq):
        cp.wait()

    def ik_chunk(c, carry):
        for i in range(pc_ik):
            page = c * pc_ik + i
            off = pl.multiple_of(page * PAGE_SIZE, PAGE_SIZE)
            bias_ref[:, pl.ds(off, PAGE_SIZE)] = page_scores(ikbuf[seq % 2, page].astype(BF16))
        return carry

    lax.fori_loop(0, n_ik_chunks, ik_chunk, 0)
    bias_ref[:, n_past:s_pad] = page_scores(kint_ref[0])

    col = lax.broadcasted_iota(I32, (ts, s_pad), 1)
    tpos = lax.broadcasted_iota(I32, (ts, s_pad), 0)
    admissible = (col - n_past) <= tpos
    bias_ref[...] = jnp.where(admissible, bias_ref[...], -jnp.inf)
    _select_topk_bias(bias_ref, code_ref, admissible, col, topk, max(1, (s_pad - 1).bit_length()), bits_per_step=2)

    qf = q_ref[0].astype(F32)
    row_head = lax.broadcasted_iota(I32, (rows, D_ATTN), 0) // ts
    lane_head = lax.broadcasted_iota(I32, (rows, D_ATTN), 1) // HEAD_DIM
    qbd = jnp.where(row_head == lane_head, jnp.concatenate([qf] * N_HEADS, axis=0), 0.0).astype(BF16)

    def attend(carry, pages_k, pages_v, bias):
        m, l, acc = carry
        n = len(pages_k) * PAGE_SIZE
        logits = jnp.concatenate([jnp.dot(qbd, kt, preferred_element_type=F32) for kt in pages_k], axis=1)
        logits = (logits.reshape(N_HEADS, ts, n) + bias[None]).reshape(rows, n)
        m_new = jnp.maximum(m, jnp.max(logits, axis=1, keepdims=True))
        alpha = jnp.exp(m - m_new)
        p = jnp.exp(logits - m_new)
        l = alpha * l + jnp.sum(p, axis=1, keepdims=True)
        p = p.astype(BF16)
        acc = alpha * acc
        for i, vt in enumerate(pages_v):
            acc = acc + lax.dot_general(p[:, i * PAGE_SIZE:(i + 1) * PAGE_SIZE], vt, NT_DIMS,
                                        preferred_element_type=F32)
        return m_new, l, acc

    n_chunk_keys = pc_kv * PAGE_SIZE

    def kv_chunk(c, carry):
        g = seq * n_kv_chunks + c
        slot = g % kv_slots

        @pl.when(g + kv_slots - 1 < n_kv_total)
        def _():
            for cp in kv_copies(g + kv_slots - 1):
                cp.start()

        for cp in kv_copies(g):
            cp.wait()
        off = pl.multiple_of(c * n_chunk_keys, n_chunk_keys)
        return attend(carry, [kbuf[slot, i].astype(BF16) for i in range(pc_kv)],
                      [vbuf[slot, i].astype(BF16) for i in range(pc_kv)], bias_ref[:, pl.ds(off, n_chunk_keys)])

    init = (jnp.full((rows, 1), SOFTMAX_FLOOR, F32), jnp.zeros((rows, 1), F32), jnp.zeros((rows, D_ATTN), F32))
    carry = lax.fori_loop(0, n_kv_chunks, kv_chunk, init)
    _, l, acc = attend(carry, [knt_ref[0]], [vnt_ref[0]], bias_ref[:, n_past:s_pad])

    acc = acc / l
    out = jnp.zeros((ts, D_ATTN), F32)
    head_of_lane = lax.broadcasted_iota(I32, (ts, D_ATTN), 1) // HEAD_DIM
    for h in range(N_HEADS):
        out = jnp.where(head_of_lane == h, acc[h * ts:(h + 1) * ts], out)
    o_ref[0] = out


def _attn_sample(layer, page_table, q_bf, knt_bf, vnt_bf, qi_rows, kint_bf, wrow, cache_kt, cache_vt, cache_ikt):
    db, ts, _ = q_bf.shape
    n_pages = page_table.shape[1]
    assert ts == SUBLANES, "one query block of 8 sublanes per sequence"
    pc_ik = _row_tile(n_pages, 32)
    pc_kv = _row_tile(n_pages, 16)
    kv_slots = 5
    s_pad = (n_pages + 1) * PAGE_SIZE
    topk = min(TOPK_MAX, (n_pages * PAGE_SIZE + ts) // 4)
    rows = N_HEADS * ts
    per_seq = lambda r, w: pl.BlockSpec((1, r, w), lambda s, pt: (s, 0, 0))
    any_spec = pl.BlockSpec(memory_space=pl.ANY)
    grid_spec = pltpu.PrefetchScalarGridSpec(
        num_scalar_prefetch=1,
        grid=(db,),
        in_specs=[per_seq(ts, D_ATTN), per_seq(D_ATTN, PAGE_SIZE), per_seq(D_ATTN, PAGE_SIZE),
                  per_seq(rows, IDX_DIM), per_seq(IDX_DIM, PAGE_SIZE), per_seq(rows, 1),
                  any_spec, any_spec, any_spec],
        out_specs=per_seq(ts, D_ATTN),
        scratch_shapes=[pltpu.VMEM((2, n_pages, IDX_DIM, PAGE_SIZE), F32),
                        pltpu.VMEM((kv_slots, pc_kv, D_ATTN, PAGE_SIZE), F32),
                        pltpu.VMEM((kv_slots, pc_kv, D_ATTN, PAGE_SIZE), F32),
                        pltpu.VMEM((ts, s_pad), F32), pltpu.VMEM((ts, s_pad), I32),
                        pltpu.SemaphoreType.DMA((2,)), pltpu.SemaphoreType.DMA((kv_slots,)),
                        pltpu.SemaphoreType.DMA((kv_slots,))])
    return pl.pallas_call(
        functools.partial(_attn_sample_body, layer=layer, n_seq=db, n_pages=n_pages, ts=ts, pc_ik=pc_ik,
                          pc_kv=pc_kv, kv_slots=kv_slots, topk=topk),
        grid_spec=grid_spec,
        out_shape=jax.ShapeDtypeStruct((db, ts, D_ATTN), F32),
        compiler_params=pltpu.CompilerParams(dimension_semantics=("arbitrary",), vmem_limit_bytes=VMEM_LIMIT),
        name="attn_sample",
    )(page_table, q_bf, knt_bf, vnt_bf, qi_rows, kint_bf, wrow, cache_kt, cache_vt, cache_ikt)


def _merge_body(x_ref, cy_ref, att_ref, g_ref, wg_ref, wc_ref, wa_ref, wo_ref, o_ref, *, d):
    x = x_ref[...]
    h = _rms(x, g_ref[...]).astype(BF16)
    gates = jax.nn.sigmoid(jnp.dot(h, wg_ref[...], preferred_element_type=F32))
    conv_out = jnp.dot(cy_ref[...].astype(BF16), wc_ref[...], preferred_element_type=F32)
    a = jnp.dot(att_ref[...].astype(BF16), wa_ref[...], preferred_element_type=F32)
    mixed = (gates[:, :d] * conv_out + gates[:, d:] * a).astype(BF16)
    o_ref[...] = x + jnp.dot(mixed, wo_ref[...], preferred_element_type=F32)


def _merge(x, conv_y, att, g, wg_bf, wc_bf, wa_bf, wo_bf):
    n, d = x.shape
    tm = _row_tile(n, 512)
    row = lambda w: pl.BlockSpec((tm, w), lambda i: (i, 0))
    return pl.pallas_call(
        functools.partial(_merge_body, d=d),
        grid=(n // tm,),
        in_specs=[row(d), row(D_CONV), row(D_ATTN), _const_spec((1, d)), _const_spec(wg_bf.shape),
                  _const_spec(wc_bf.shape), _const_spec(wa_bf.shape), _const_spec(wo_bf.shape)],
        out_specs=row(d),
        out_shape=jax.ShapeDtypeStruct((n, d), F32),
        compiler_params=pltpu.CompilerParams(dimension_semantics=("parallel",), vmem_limit_bytes=VMEM_LIMIT),
        name="merge",
    )(x, conv_y, att, g, wg_bf, wc_bf, wa_bf, wo_bf)


def _rope_tables(pos, head_dim, n_heads):
    half = head_dim // 2
    inv = jnp.float32(ROPE_THETA) ** (-jnp.arange(half, dtype=F32) * (2.0 / head_dim))
    ang = pos.astype(F32)[:, None] * inv[None, :]
    cos = jnp.tile(jnp.concatenate([jnp.cos(ang), jnp.cos(ang)], axis=1), (1, n_heads))
    sin = jnp.tile(jnp.concatenate([-jnp.sin(ang), jnp.sin(ang)], axis=1), (1, n_heads))
    return cos, sin


def kernel(x_prompt, x_sample, cache_k, cache_v, cache_idx_k, state_conv, page_table, g_ffn1, w_ffn1_in, w_ffn1_out, g_mix, w_in, g_q, g_k, w_dw, b_dw, g_cln, b_cln, w_cout, w_aout, w_o, g_ffn2, w_ffn2_in, w_ffn2_out):
    b, t, d = x_prompt.shape
    db, ts, _ = x_sample.shape
    depth = g_mix.shape[0]
    n_pages = page_table.shape[1]
    n_phys = cache_k.shape[1]
    past_len = n_pages * PAGE_SIZE
    hist_rows = CONV_W - 1

    pos_p = jnp.arange(t)
    pos_s = jnp.tile(past_len + jnp.arange(ts), db)
    tab_p = _rope_tables(pos_p, HEAD_DIM, N_HEADS) + _rope_tables(pos_p, IDX_DIM, N_IDX_HEADS)
    tab_s = _rope_tables(pos_s, HEAD_DIM, N_HEADS) + _rope_tables(pos_s, IDX_DIM, N_IDX_HEADS)
    seg = jnp.kron(jnp.eye(N_HEADS, dtype=F32), jnp.full((HEAD_DIM, HEAD_DIM), 1.0 / HEAD_DIM, F32)).astype(BF16)

    xp = x_prompt.reshape(b * t, d)
    xs = x_sample.reshape(db * ts, d)
    outs = [[] for _ in range(8)]
    for l in range(depth):
        vec = lambda a: a[l][None, :]
        w1i, w1o = w_ffn1_in[l].astype(BF16), w_ffn1_out[l].astype(BF16)
        w2i, w2o = w_ffn2_in[l].astype(BF16), w_ffn2_out[l].astype(BF16)
        c_main = 2 * D_CONV + 3 * D_ATTN + D_IDX
        w_ki = w_in[l][:, c_main:c_main + IDX_DIM]
        w_wi = w_in[l][:, c_main + IDX_DIM:c_main + IDX_DIM + N_IDX_HEADS]
        w_proj = jnp.concatenate([w_in[l][:, :c_main], jnp.tile(w_ki, (1, N_IDX_HEADS)),
                                  jnp.pad(w_wi, ((0, 0), (0, LANES - N_IDX_HEADS)))], axis=1).astype(BF16)
        w_gates = w_in[l][:, c_main + IDX_DIM + N_IDX_HEADS:].astype(BF16)
        gq = jnp.tile(g_q[l], N_HEADS)[None, :]
        gk = jnp.tile(g_k[l], N_HEADS)[None, :]
        wc, wa, wo = w_cout[l].astype(BF16), w_aout[l].astype(BF16), w_o[l].astype(BF16)

        xp = _ffn(xp, vec(g_ffn1), w1i, w1o)
        xs = _ffn(xs, vec(g_ffn1), w1i, w1o)
        up, kp, vp, kip, qbp, kbp, vbp, qibp, ki8bp, wip = _proj(xp, vec(g_mix), w_proj, gq, gk, seg, tab_p, t, True)
        us, ks, vs, kis, qbs, kbs, vbs, qibs, ki8bs, wis = _proj(xs, vec(g_mix), w_proj, gq, gk, seg, tab_s, db * ts,
                                                                 False)

        up3, us3 = up.reshape(b, t, D_CONV), us.reshape(db, ts, D_CONV)
        lead = ((0, 0), (HALO - hist_rows, 0), (0, 0))
        hist_p = jnp.zeros((b, HALO, D_CONV), F32)
        hist_s = jnp.pad(state_conv[l], lead)
        conv_args = (w_dw[l], vec(b_dw), vec(g_cln), vec(b_cln))
        cyp = _conv(up3, hist_p, *conv_args)
        cys = _conv(us3, hist_s, *conv_args)
        st_p = jnp.concatenate([jnp.zeros((b, hist_rows, D_CONV), F32), up3], axis=1)[:, -hist_rows:]
        st_s = jnp.concatenate([state_conv[l], us3], axis=1)[:, -hist_rows:]

        r3 = lambda a, n, w: a.reshape(n, -1, w)
        att_p = _attn_prompt(r3(qbp, b, D_ATTN), r3(kbp, b, D_ATTN), r3(vbp, b, D_ATTN),
                             r3(qibp, b, D_IDX), r3(ki8bp, b, D_IDX), r3(wip, b, LANES))
        head_major = lambda a, nh, w: a.reshape(db, ts, nh, w).transpose(0, 2, 1, 3).reshape(db, nh * ts, w)
        new_page_t = lambda a: jnp.pad(a.reshape(db, ts, -1).transpose(0, 2, 1), ((0, 0), (0, 0), (0, PAGE_SIZE - ts)))
        att_s = _attn_sample(
            l, page_table, r3(qbs, db, D_ATTN), new_page_t(kbs), new_page_t(vbs),
            head_major(qibs, N_IDX_HEADS, IDX_DIM), new_page_t(ki8bs[:, :IDX_DIM]),
            head_major(wis[:, :N_IDX_HEADS], N_IDX_HEADS, 1),
            cache_k.transpose(0, 1, 3, 4, 2).reshape(depth, n_phys, D_ATTN, PAGE_SIZE),
            cache_v.transpose(0, 1, 3, 4, 2).reshape(depth, n_phys, D_ATTN, PAGE_SIZE),
            cache_idx_k.transpose(0, 1, 3, 2))

        xp = _merge(xp, cyp.reshape(b * t, D_CONV), att_p.reshape(b * t, D_ATTN), vec(g_mix), w_gates, wc, wa, wo)
        xs = _merge(xs, cys.reshape(db * ts, D_CONV), att_s.reshape(db * ts, D_ATTN), vec(g_mix), w_gates, wc, wa, wo)
        xp = _ffn(xp, vec(g_ffn2), w2i, w2o)
        xs = _ffn(xs, vec(g_ffn2), w2i, w2o)

        token_major = lambda a: a.reshape(b, N_HEADS, HEAD_DIM, t).transpose(0, 3, 1, 2)
        for lst, val in zip(outs, (token_major(kp), token_major(vp), kip.transpose(0, 2, 1), st_p,
                                   ks.reshape(db, ts, N_HEADS, HEAD_DIM), vs.reshape(db, ts, N_HEADS, HEAD_DIM),
                                   kis.reshape(db, ts, IDX_DIM), st_s)):
            lst.append(val)

    return (xp.reshape(b, t, d), xs.reshape(db, ts, d)) + tuple(jnp.stack(o) for o in outs)
```

```python
import functools
import math

import jax
import jax.numpy as jnp
from jax import lax
from jax.experimental import pallas as pl
from jax.experimental.pallas import tpu as pltpu

N_HEADS = 8
HEAD_DIM = 64
D_ATTN = N_HEADS * HEAD_DIM
D_CONV = 512
CONV_W = 31
N_IDX_HEADS = 8
IDX_DIM = 32
D_IDX = N_IDX_HEADS * IDX_DIM
TOPK_MAX = 256
PAGE_SIZE = 128
ROPE_THETA = 10000.0
EPS = 1e-6
IDX_W_SCALE = D_IDX ** -0.5

LANES = 128
SUBLANES = 8
MXU_TILE = 256
LOG2E = math.log2(math.e)
HALO = 32
INT_MIN = -2 ** 31
NEG_INF_KEY = INT_MIN + 0x7FFFFF
MASKED_LOGIT = -1e30
SOFTMAX_FLOOR = 0.25 * MASKED_LOGIT
COUNT_FOLD = 16 * LANES
VMEM_LIMIT = 56 * 1024 * 1024

F32 = jnp.float32
BF16 = jnp.bfloat16
I32 = jnp.int32
NT_DIMS = (((1,), (1,)), ((), ()))


def _row_tile(n, cap):
    t = cap
    while n % t:
        t //= 2
    return t


def _rms(x, g):
    return x * lax.rsqrt(jnp.mean(x * x, axis=-1, keepdims=True) + EPS) * g


def _const_spec(shape):
    nd = len(shape)
    return pl.BlockSpec(shape, lambda *_: (0,) * nd, pipeline_mode=pl.Buffered(1))


def _ffn_body(x_ref, g_ref, win_ref, wout_ref, o_ref, *, d_ff, n_chunks):
    x = x_ref[...]
    h = _rms(x, g_ref[...]).astype(BF16)
    fc = d_ff // n_chunks
    acc = jnp.zeros_like(x)
    for c in range(n_chunks):
        gate = jnp.dot(h, win_ref[:, c * fc:(c + 1) * fc], preferred_element_type=F32)
        up = jnp.dot(h, win_ref[:, d_ff + c * fc:d_ff + (c + 1) * fc], preferred_element_type=F32)
        act = (gate * jax.nn.sigmoid(gate) * up).astype(BF16)
        acc = acc + jnp.dot(act, wout_ref[c * fc:(c + 1) * fc, :], preferred_element_type=F32)
    o_ref[...] = x + 0.5 * acc


def _ffn(x, g, w_in_bf, w_out_bf):
    n, d = x.shape
    d_ff = w_out_bf.shape[0]
    tm = _row_tile(n, 512)
    n_chunks = 2 if d_ff % (2 * LANES) == 0 else 1
    return pl.pallas_call(
        functools.partial(_ffn_body, d_ff=d_ff, n_chunks=n_chunks),
        grid=(n // tm,),
        in_specs=[pl.BlockSpec((tm, d), lambda i: (i, 0)),
                  _const_spec((1, d)), _const_spec((d, 2 * d_ff)), _const_spec((d_ff, d))],
        out_specs=pl.BlockSpec((tm, d), lambda i: (i, 0)),
        out_shape=jax.ShapeDtypeStruct((n, d), F32),
        compiler_params=pltpu.CompilerParams(dimension_semantics=("parallel",), vmem_limit_bytes=VMEM_LIMIT),
        name="ffn",
    )(x, g, w_in_bf, w_out_bf)


def _rope(x, cos, sin_signed, half):
    n = x.shape[-1]
    lane = lax.broadcasted_iota(I32, x.shape, 1)
    first = (lane % (2 * half)) < half
    partner = jnp.where(first, pltpu.roll(x, n - half, 1), pltpu.roll(x, half, 1))
    return x * cos + partner * sin_signed


def _head_rms(x, seg_ref, g):
    x2 = x * x
    hi = x2.astype(BF16)
    lo = (x2 - hi.astype(F32)).astype(BF16)
    w = seg_ref.shape[0]
    ms = jnp.concatenate([jnp.dot(hi[:, c:c + w], seg_ref[...], preferred_element_type=F32)
                          + jnp.dot(lo[:, c:c + w], seg_ref[...], preferred_element_type=F32)
                          for c in range(0, x.shape[1], w)], axis=1)
    return x * lax.rsqrt(ms + EPS) * g


def _proj_body(x_ref, g_ref, w_ref, gq_ref, gk_ref, seg_ref, cqk_ref, sqk_ref, ci_ref, si_ref,
               u_ref, k_ref, v_ref, ki_ref, qb_ref, kb_ref, vb_ref, qib_ref, ki8b_ref, wi_ref, *, feature_major):
    h = _rms(x_ref[...], g_ref[...]).astype(BF16)
    p = jnp.dot(h, w_ref[...], preferred_element_type=F32)
    c0 = 0
    a_glu = p[:, c0:c0 + D_CONV]; c0 += D_CONV
    b_glu = p[:, c0:c0 + D_CONV]; c0 += D_CONV
    q = p[:, c0:c0 + D_ATTN]; c0 += D_ATTN
    k = p[:, c0:c0 + D_ATTN]; c0 += D_ATTN
    v = p[:, c0:c0 + D_ATTN]; c0 += D_ATTN
    qi = p[:, c0:c0 + D_IDX]; c0 += D_IDX
    ki8 = p[:, c0:c0 + D_IDX]; c0 += D_IDX
    wi = p[:, c0:c0 + LANES]

    u_ref[...] = a_glu * jax.nn.sigmoid(b_glu)
    q = _rope(_head_rms(q, seg_ref, gq_ref[...]), cqk_ref[...], sqk_ref[...], HEAD_DIM // 2)
    k = _rope(_head_rms(k, seg_ref, gk_ref[...]), cqk_ref[...], sqk_ref[...], HEAD_DIM // 2)
    qi = _rope(qi, ci_ref[...], si_ref[...], IDX_DIM // 2)
    ki8 = _rope(ki8, ci_ref[...], si_ref[...], IDX_DIM // 2)

    if feature_major:
        k_ref[0] = k.T
        v_ref[0] = v.T
        ki_ref[0] = ki8.T[:IDX_DIM, :]
    else:
        k_ref[...] = k
        v_ref[...] = v
        ki_ref[...] = ki8[:, :IDX_DIM]
    qb_ref[...] = (q * (HEAD_DIM ** -0.5 * LOG2E)).astype(BF16)
    kb_ref[...] = k.astype(BF16)
    vb_ref[...] = v.astype(BF16)
    qib_ref[...] = qi.astype(BF16)
    ki8b_ref[...] = ki8.astype(BF16)
    wi_ref[...] = wi * IDX_W_SCALE


def _proj(x, g, w_bf, gq, gk, seg, tables, tab_rows, feature_major):
    n, d = x.shape
    tm = _row_tile(math.gcd(n, tab_rows), 256)
    tab_blocks = tab_rows // tm
    row = lambda w: pl.BlockSpec((tm, w), lambda i: (i, 0))
    tab = lambda w: pl.BlockSpec((tm, w), lambda i: (i % tab_blocks, 0))
    f32o = lambda w: jax.ShapeDtypeStruct((n, w), F32)
    bfo = lambda w: jax.ShapeDtypeStruct((n, w), BF16)
    if feature_major:
        cache = lambda w: pl.BlockSpec((1, w, tm), lambda i: (i // tab_blocks, 0, i % tab_blocks))
        cacheo = lambda w: jax.ShapeDtypeStruct((n // tab_rows, w, tab_rows), F32)
    else:
        cache, cacheo = row, f32o
    return pl.pallas_call(
        functools.partial(_proj_body, feature_major=feature_major),
        grid=(n // tm,),
        in_specs=[row(d), _const_spec((1, d)), _const_spec(w_bf.shape),
                  _const_spec((1, D_ATTN)), _const_spec((1, D_ATTN)), _const_spec(seg.shape),
                  tab(D_ATTN), tab(D_ATTN), tab(D_IDX), tab(D_IDX)],
        out_specs=[row(D_CONV), cache(D_ATTN), cache(D_ATTN), cache(IDX_DIM),
                   row(D_ATTN), row(D_ATTN), row(D_ATTN), row(D_IDX), row(D_IDX), row(LANES)],
        out_shape=[f32o(D_CONV), cacheo(D_ATTN), cacheo(D_ATTN), cacheo(IDX_DIM),
                   bfo(D_ATTN), bfo(D_ATTN), bfo(D_ATTN), bfo(D_IDX), bfo(D_IDX), f32o(LANES)],
        compiler_params=pltpu.CompilerParams(dimension_semantics=("parallel",), vmem_limit_bytes=VMEM_LIMIT),
        name="proj",
    )(x, g, w_bf, gq, gk, seg, *tables)


def _conv_body(u_ref, uprev_ref, hist_ref, w_ref, b_ref, g_ref, bl_ref, o_ref, xs_ref, sh_ref, *, tc, n_t):
    xs_ref[HALO:HALO + tc, :] = u_ref[0]
    xs_ref[HALO + tc:HALO + tc + SUBLANES, :] = jnp.zeros((SUBLANES, D_CONV), F32)
    if n_t == 1:
        xs_ref[0:HALO, :] = hist_ref[0]
    else:
        i = pl.program_id(1)

        @pl.when(i == 0)
        def _():
            xs_ref[0:HALO, :] = hist_ref[0]

        @pl.when(i > 0)
        def _():
            xs_ref[0:HALO, :] = uprev_ref[0, tc - HALO:tc, :]

    n_sh = sh_ref.shape[1]
    for s in range(SUBLANES):
        sh_ref[s] = xs_ref[s:s + n_sh, :]

    rb = min(tc, 32)
    lead = HALO - (CONV_W - 1)
    for r0 in range(0, tc, rb):
        acc = jnp.broadcast_to(b_ref[...], (rb, D_CONV))
        for j in range(CONV_W):
            s = (lead + j) % SUBLANES
            base = r0 + lead + j - s
            acc = acc + w_ref[j:j + 1, :] * sh_ref[s, base:base + rb, :]
        mu = jnp.mean(acc, axis=-1, keepdims=True)
        yc = acc - mu
        y = yc * lax.rsqrt(jnp.mean(yc * yc, axis=-1, keepdims=True) + EPS) * g_ref[...] + bl_ref[...]
        o_ref[0, r0:r0 + rb, :] = y * jax.nn.sigmoid(y)


def _conv(u, hist, w_dw, b_dw, g_ln, b_ln):
    b, t, c = u.shape
    tc = _row_tile(t, 256)
    n_t = t // tc
    assert n_t == 1 or tc >= HALO
    vec = _const_spec((1, c))
    return pl.pallas_call(
        functools.partial(_conv_body, tc=tc, n_t=n_t),
        grid=(b, n_t),
        in_specs=[pl.BlockSpec((1, tc, c), lambda bi, i: (bi, i, 0)),
                  pl.BlockSpec((1, tc, c), lambda bi, i: (bi, jnp.maximum(i - 1, 0), 0)),
                  pl.BlockSpec((1, HALO, c), lambda bi, i: (bi, 0, 0)),
                  _const_spec((CONV_W, c)), vec, vec, vec],
        out_specs=pl.BlockSpec((1, tc, c), lambda bi, i: (bi, i, 0)),
        out_shape=jax.ShapeDtypeStruct((b, t, c), F32),
        scratch_shapes=[pltpu.VMEM((HALO + tc + SUBLANES, c), F32),
                        pltpu.VMEM((SUBLANES, HALO + tc, c), F32)],
        compiler_params=pltpu.CompilerParams(dimension_semantics=("parallel", "parallel")),
        name="conv",
    )(u, u, hist, w_dw, b_dw, g_ln, b_ln)


def _rank_to_f32(u):
    key = u + NEG_INF_KEY
    return lax.bitcast_convert_type(key ^ ((key >> 31) & 0x7FFFFFFF), F32)


def _count(mask):
    ones = jnp.where(mask, 1.0, 0.0)
    n = ones.shape[1]
    if n > COUNT_FOLD:
        parts = [ones[:, i:i + COUNT_FOLD] for i in range(0, n - n % COUNT_FOLD, COUNT_FOLD)]
        while len(parts) > 1:
            parts = [a + b for a, b in zip(parts[0::2], parts[1::2])] + parts[len(parts) & ~1:]
        total = jnp.sum(parts[0], axis=1, keepdims=True)
        if n % COUNT_FOLD:
            total = total + jnp.sum(ones[:, n - n % COUNT_FOLD:], axis=1, keepdims=True)
        return total
    return jnp.sum(ones, axis=1, keepdims=True)


def _select_topk_bias(sc_ref, code_ref, admissible, col, topk, n_idx_bits, bits_per_step):
    rows = sc_ref.shape[0]
    n_cand = 2 ** bits_per_step - 1
    assert 32 % bits_per_step == 0
    n_idx_bits = -(-n_idx_bits // bits_per_step) * bits_per_step

    def value_step(i, u):
        shift = 32 - bits_per_step * (i + 1)
        unit = jnp.int32(INT_MIN if shift == 31 else 1 << shift)
        sc = sc_ref[...]
        taken = jnp.zeros((rows, 1), I32)
        for j in range(1, n_cand + 1):
            taken = taken + jnp.where(_count(sc >= _rank_to_f32(u + j * unit)) >= topk, 1, 0)
        return u + taken * unit

    u = jnp.zeros((rows, 1), I32)
    for i in range(32 // bits_per_step):
        u = value_step(i, u)
    thr = _rank_to_f32(u)

    sc = sc_ref[...]
    code_ref[...] = jnp.where(sc > thr, -1, jnp.where(sc == thr, col, 2 ** n_idx_bits))

    def index_step(i, bound):
        unit = 1 << (n_idx_bits - bits_per_step * (i + 1))
        code = code_ref[...]
        taken = jnp.zeros((rows, 1), I32)
        for j in range(1, n_cand + 1):
            taken = taken + jnp.where(_count(code < bound + j * unit) < topk, 1, 0)
        return bound + taken * unit

    bound = jnp.zeros((rows, 1), I32)
    for i in range(n_idx_bits // bits_per_step):
        bound = index_step(i, bound)
    sc_ref[...] = jnp.where(admissible & (code_ref[...] <= bound), 0.0, MASKED_LOGIT)


def _pattern_to_f32(u, bf16_exact=False):
    key = jnp.maximum(u ^ INT_MIN, NEG_INF_KEY)
    bits = key ^ ((key >> 31) & 0x7FFFFFFF)
    if bf16_exact:
        bits = bits & -(1 << 16)
    return lax.bitcast_convert_type(bits, F32)


def _f32_to_pattern(x):
    bits = lax.bitcast_convert_type(x, I32)
    return (bits ^ ((bits >> 31) & 0x7FFFFFFF)) ^ INT_MIN


def _select_topk_bias_dense(sc_ref, sb_ref, tri_ref, topk):
    rows, n = sc_ref.shape

    def unrolled(n_steps, body, carry):
        for i in range(n_steps):
            carry = body(i, carry)
        return carry
    one, zero = jnp.ones((), BF16), jnp.zeros((), BF16)
    sb_ref[...] = sc_ref[...].astype(BF16)

    def count_bf16(cand):
        cb = jnp.broadcast_to(cand, (rows, LANES)).astype(BF16)
        acc = jnp.where(sb_ref[:, 0:LANES] >= cb, one, zero)
        for j in range(1, n // LANES):
            acc = acc + jnp.where(sb_ref[:, j * LANES:(j + 1) * LANES] >= cb, one, zero)
        return jnp.sum(acc.astype(F32), axis=1, keepdims=True)

    def coarse_step(i, p):
        cand = p + (1 << (15 - i))
        return jnp.where(count_bf16(_pattern_to_f32(cand << 16, bf16_exact=True)) >= topk, cand, p)

    p = unrolled(16, coarse_step, jnp.zeros((rows, 1), I32))
    base = _f32_to_pattern(_pattern_to_f32(p << 16, bf16_exact=True)) - (1 << 15)

    def fine_step(i, d):
        cand = d + (1 << (16 - i))
        return jnp.where(_count(sc_ref[...] >= _pattern_to_f32(base + cand)) >= topk, cand, d)

    thr = _pattern_to_f32(base + unrolled(17, fine_step, jnp.zeros((rows, 1), I32)))

    need = jnp.where(thr == -jnp.inf, 0.0, topk - _count(sc_ref[...] > thr))
    cw = tri_ref.shape[0]
    seen = jnp.zeros((rows, 1), F32)
    for c in range(n // cw):
        s = sc_ref[:, c * cw:(c + 1) * cw]
        tie = jnp.where(s == thr, 1.0, 0.0)
        earlier = jnp.dot(tie.astype(BF16), tri_ref[...], preferred_element_type=F32) + seen
        keep_tie = jnp.where(earlier <= need, 0.0, MASKED_LOGIT)
        sc_ref[:, c * cw:(c + 1) * cw] = jnp.where(s > thr, 0.0, jnp.where(s == thr, keep_tie, MASKED_LOGIT))
        seen = seen + jnp.sum(tie, axis=1, keepdims=True)


def _attn_prompt_body(q_ref, k_ref, v_ref, qi_ref, ki8_ref, wi_ref, tri_ref, o_ref, bias_ref, sb_ref,
                      *, tq, q0, topk):
    t = q0 + tq
    qi = qi_ref[0]
    ki8 = ki8_ref[0]
    wi = wi_ref[0]
    idx_head = lax.broadcasted_iota(I32, (1, D_IDX), 1) // IDX_DIM
    scores = jnp.zeros((tq, t), F32)
    for h in range(N_IDX_HEADS):
        qh = qi * jnp.where(idx_head == h, 1.0, 0.0).astype(BF16)
        dots = lax.dot_general(qh, ki8, NT_DIMS, preferred_element_type=F32)
        scores = scores + wi[:, h:h + 1] * jnp.maximum(dots, 0.0)

    col = lax.broadcasted_iota(I32, (tq, t), 1)
    qpos = q0 + lax.broadcasted_iota(I32, (tq, t), 0)
    bias_ref[...] = jnp.where(col <= qpos, scores, -jnp.inf)
    _select_topk_bias_dense(bias_ref, sb_ref, tri_ref, topk)

    pair_head = lax.broadcasted_iota(I32, (1, LANES), 1) // HEAD_DIM
    for j in range(D_ATTN // LANES):
        sl = slice(j * LANES, (j + 1) * LANES)
        qj = q_ref[0, :, sl]
        kj = k_ref[0, :, sl]
        vj = v_ref[0, :, sl]
        out = jnp.zeros((tq, LANES), F32)
        for half in range(LANES // HEAD_DIM):
            mine = pair_head == half
            qh = qj * jnp.where(mine, 1.0, 0.0).astype(BF16)
            logits = lax.dot_general(qh, kj, NT_DIMS, preferred_element_type=F32) + bias_ref[...]
            m = jnp.max(logits, axis=1, keepdims=True)
            p = jnp.exp2(logits - m)
            denom = jnp.sum(p, axis=1, keepdims=True)
            o = jnp.dot(p.astype(BF16), vj, preferred_element_type=F32) / denom
            out = jnp.where(mine, o, out)
        o_ref[0, :, sl] = out


def _attn_prompt(q_bf, k_bf, v_bf, qi_bf, ki8_bf, wi):
    b, t, _ = q_bf.shape
    tq = _row_tile(t, 256)
    topk = min(TOPK_MAX, t // 4)
    cw = min(tq, 2 * LANES)
    assert tq % cw == 0 and cw % LANES == 0
    tri = jnp.triu(jnp.ones((cw, cw), BF16))
    blocks = []
    for qb in range(t // tq):
        kv = (qb + 1) * tq
        qspec = lambda w, qb=qb: pl.BlockSpec((1, tq, w), lambda bi: (bi, qb, 0))
        kspec = lambda w, kv=kv: pl.BlockSpec((1, kv, w), lambda bi: (bi, 0, 0))
        blocks.append(pl.pallas_call(
            functools.partial(_attn_prompt_body, tq=tq, q0=qb * tq, topk=topk),
            grid=(b,),
            in_specs=[qspec(D_ATTN), kspec(D_ATTN), kspec(D_ATTN), qspec(D_IDX), kspec(D_IDX), qspec(LANES),
                      pl.BlockSpec((cw, cw), lambda bi: (0, 0))],
            out_specs=pl.BlockSpec((1, tq, D_ATTN), lambda bi: (bi, 0, 0)),
            out_shape=jax.ShapeDtypeStruct((b, tq, D_ATTN), F32),
            scratch_shapes=[pltpu.VMEM((tq, kv), F32), pltpu.VMEM((tq, kv), BF16)],
            compiler_params=pltpu.CompilerParams(dimension_semantics=("parallel",), vmem_limit_bytes=VMEM_LIMIT),
            name=f"attn_prompt_q{qb}",
        )(q_bf, k_bf, v_bf, qi_bf, ki8_bf, wi, tri))
    return jnp.concatenate(blocks, axis=1)


def _attn_sample_body(pt_ref, q_ref, knt_ref, vnt_ref, qi_ref, kint_ref, wrow_ref, ck_hbm, cv_hbm, cik_hbm,
                      o_ref, ikbuf, kbuf, vbuf, bias_ref, code_ref, sem_ik, sem_k, sem_v,
                      *, layer, n_seq, n_pages, ts, pc_ik, pc_kv, kv_slots, topk):
    seq = pl.program_id(0)
    n_past = n_pages * PAGE_SIZE
    s_pad = n_past + PAGE_SIZE
    rows = N_HEADS * ts
    n_ik_chunks = n_pages // pc_ik
    n_kv_chunks = n_pages // pc_kv
    n_kv_total = n_seq * n_kv_chunks

    def page_copy(hbm, buf, sem, s, page, slot, i):
        return pltpu.make_async_copy(hbm.at[layer, pt_ref[s, page]], buf.at[slot, i], sem.at[slot])

    def ik_copies(s):
        return [page_copy(cik_hbm, ikbuf, sem_ik, s, p, s % 2, p) for p in range(n_pages)]

    def kv_copies(g):
        s, c = g // n_kv_chunks, g % n_kv_chunks
        slot = g % kv_slots
        return [page_copy(hbm, buf, sem, s, c * pc_kv + i, slot, i)
                for hbm, buf, sem in ((ck_hbm, kbuf, sem_k), (cv_hbm, vbuf, sem_v)) for i in range(pc_kv)]

    @pl.when(seq == 0)
    def _():
        for cp in ik_copies(0):
            cp.start()
        for g in range(min(kv_slots - 1, n_kv_total)):
            for cp in kv_copies(g):
                cp.start()

    @pl.when(seq + 1 < n_seq)
    def _():
        for cp in ik_copies(seq + 1):
            cp.start()

    qi = qi_ref[0]
    wrow = wrow_ref[0]

    def page_scores(keys_t_bf):
        r = jnp.maximum(jnp.dot(qi, keys_t_bf, preferred_element_type=F32), 0.0) * wrow
        s = r[0:ts]
        for h in range(1, N_IDX_HEADS):
            s = s + r[h * ts:(h + 1) * ts]
        return s

    for cp in ik_copies(seq):
        cp.wait()

    def ik_chunk(c, carry):
        for i in range(pc_ik):
            page = c * pc_ik + i
            off = pl.multiple_of(page * PAGE_SIZE, PAGE_SIZE)
            bias_ref[:, pl.ds(off, PAGE_SIZE)] = page_scores(ikbuf[seq % 2, page].astype(BF16))
        return carry

    lax.fori_loop(0, n_ik_chunks, ik_chunk, 0)
    bias_ref[:, n_past:s_pad] = page_scores(kint_ref[0])

    col = lax.broadcasted_iota(I32, (ts, s_pad), 1)
    tpos = lax.broadcasted_iota(I32, (ts, s_pad), 0)
    admissible = (col - n_past) <= tpos
    bias_ref[...] = jnp.where(admissible, bias_ref[...], -jnp.inf)
    _select_topk_bias(bias_ref, code_ref, admissible, col, topk, max(1, (s_pad - 1).bit_length()), bits_per_step=2)

    qf = q_ref[0].astype(F32)
    row_head = lax.broadcasted_iota(I32, (rows, D_ATTN), 0) // ts
    lane_head = lax.broadcasted_iota(I32, (rows, D_ATTN), 1) // HEAD_DIM
    qbd = jnp.where(row_head == lane_head, jnp.concatenate([qf] * N_HEADS, axis=0), 0.0).astype(BF16)

    def attend(carry, pages_k, pages_v, bias):
        m, l, acc = carry
        n = len(pages_k) * PAGE_SIZE
        logits = jnp.concatenate([jnp.dot(qbd, kt, preferred_element_type=F32) for kt in pages_k], axis=1)
        logits = (logits.reshape(N_HEADS, ts, n) + bias[None]).reshape(rows, n)
        m_new = jnp.maximum(m, jnp.max(logits, axis=1, keepdims=True))
        alpha = jnp.exp2(m - m_new)
        p = jnp.exp2(logits - m_new)
        l = alpha * l + jnp.sum(p, axis=1, keepdims=True)
        p = p.astype(BF16)
        acc = alpha * acc
        for i, vt in enumerate(pages_v):
            acc = acc + lax.dot_general(p[:, i * PAGE_SIZE:(i + 1) * PAGE_SIZE], vt, NT_DIMS,
                                        preferred_element_type=F32)
        return m_new, l, acc

    n_chunk_keys = pc_kv * PAGE_SIZE

    def kv_chunk(c, carry):
        g = seq * n_kv_chunks + c
        slot = g % kv_slots

        @pl.when(g + kv_slots - 1 < n_kv_total)
        def _():
            for cp in kv_copies(g + kv_slots - 1):
                cp.start()

        for cp in kv_copies(g):
            cp.wait()
        off = pl.multiple_of(c * n_chunk_keys, n_chunk_keys)
        return attend(carry, [kbuf[slot, i].astype(BF16) for i in range(pc_kv)],
                      [vbuf[slot, i].astype(BF16) for i in range(pc_kv)], bias_ref[:, pl.ds(off, n_chunk_keys)])

    init = (jnp.full((rows, 1), SOFTMAX_FLOOR, F32), jnp.zeros((rows, 1), F32), jnp.zeros((rows, D_ATTN), F32))
    carry = lax.fori_loop(0, n_kv_chunks, kv_chunk, init)
    _, l, acc = attend(carry, [knt_ref[0]], [vnt_ref[0]], bias_ref[:, n_past:s_pad])

    acc = acc / l
    out = jnp.zeros((ts, D_ATTN), F32)
    head_of_lane = lax.broadcasted_iota(I32, (ts, D_ATTN), 1) // HEAD_DIM
    for h in range(N_HEADS):
        out = jnp.where(head_of_lane == h, acc[h * ts:(h + 1) * ts], out)
    o_ref[0] = out


def _attn_sample(layer, page_table, q_bf, knt_bf, vnt_bf, qi_rows, kint_bf, wrow, cache_kt, cache_vt, cache_ikt):
    db, ts, _ = q_bf.shape
    n_pages = page_table.shape[1]
    assert ts == SUBLANES, "one query block of 8 sublanes per sequence"
    pc_ik = _row_tile(n_pages, 32)
    pc_kv = _row_tile(n_pages, 16)
    kv_slots = 5
    s_pad = (n_pages + 1) * PAGE_SIZE
    topk = min(TOPK_MAX, (n_pages * PAGE_SIZE + ts) // 4)
    rows = N_HEADS * ts
    per_seq = lambda r, w: pl.BlockSpec((1, r, w), lambda s, pt: (s, 0, 0))
    any_spec = pl.BlockSpec(memory_space=pl.ANY)
    grid_spec = pltpu.PrefetchScalarGridSpec(
        num_scalar_prefetch=1,
        grid=(db,),
        in_specs=[per_seq(ts, D_ATTN), per_seq(D_ATTN, PAGE_SIZE), per_seq(D_ATTN, PAGE_SIZE),
                  per_seq(rows, IDX_DIM), per_seq(IDX_DIM, PAGE_SIZE), per_seq(rows, 1),
                  any_spec, any_spec, any_spec],
        out_specs=per_seq(ts, D_ATTN),
        scratch_shapes=[pltpu.VMEM((2, n_pages, IDX_DIM, PAGE_SIZE), F32),
                        pltpu.VMEM((kv_slots, pc_kv, D_ATTN, PAGE_SIZE), F32),
                        pltpu.VMEM((kv_slots, pc_kv, D_ATTN, PAGE_SIZE), F32),
                        pltpu.VMEM((ts, s_pad), F32), pltpu.VMEM((ts, s_pad), I32),
                        pltpu.SemaphoreType.DMA((2,)), pltpu.SemaphoreType.DMA((kv_slots,)),
                        pltpu.SemaphoreType.DMA((kv_slots,))])
    return pl.pallas_call(
        functools.partial(_attn_sample_body, layer=layer, n_seq=db, n_pages=n_pages, ts=ts, pc_ik=pc_ik,
                          pc_kv=pc_kv, kv_slots=kv_slots, topk=topk),
        grid_spec=grid_spec,
        out_shape=jax.ShapeDtypeStruct((db, ts, D_ATTN), F32),
        compiler_params=pltpu.CompilerParams(dimension_semantics=("arbitrary",), vmem_limit_bytes=VMEM_LIMIT),
        name="attn_sample",
    )(page_table, q_bf, knt_bf, vnt_bf, qi_rows, kint_bf, wrow, cache_kt, cache_vt, cache_ikt)


def _merge_body(x_ref, cy_ref, att_ref, g_ref, wg_ref, wc_ref, wa_ref, wo_ref, o_ref, *, d):
    x = x_ref[...]
    h = _rms(x, g_ref[...]).astype(BF16)
    gates = jax.nn.sigmoid(jnp.dot(h, wg_ref[...], preferred_element_type=F32))
    conv_out = jnp.dot(cy_ref[...].astype(BF16), wc_ref[...], preferred_element_type=F32)
    a = jnp.dot(att_ref[...].astype(BF16), wa_ref[...], preferred_element_type=F32)
    mixed = (gates[:, :d] * conv_out + gates[:, d:] * a).astype(BF16)
    o_ref[...] = x + jnp.dot(mixed, wo_ref[...], preferred_element_type=F32)


def _merge(x, conv_y, att, g, wg_bf, wc_bf, wa_bf, wo_bf):
    n, d = x.shape
    tm = _row_tile(n, 512)
    row = lambda w: pl.BlockSpec((tm, w), lambda i: (i, 0))
    return pl.pallas_call(
        functools.partial(_merge_body, d=d),
        grid=(n // tm,),
        in_specs=[row(d), row(D_CONV), row(D_ATTN), _const_spec((1, d)), _const_spec(wg_bf.shape),
                  _const_spec(wc_bf.shape), _const_spec(wa_bf.shape), _const_spec(wo_bf.shape)],
        out_specs=row(d),
        out_shape=jax.ShapeDtypeStruct((n, d), F32),
        compiler_params=pltpu.CompilerParams(dimension_semantics=("parallel",), vmem_limit_bytes=VMEM_LIMIT),
        name="merge",
    )(x, conv_y, att, g, wg_bf, wc_bf, wa_bf, wo_bf)


def _rope_tables(pos, head_dim, n_heads):
    half = head_dim // 2
    inv = jnp.float32(ROPE_THETA) ** (-jnp.arange(half, dtype=F32) * (2.0 / head_dim))
    ang = pos.astype(F32)[:, None] * inv[None, :]
    cos = jnp.tile(jnp.concatenate([jnp.cos(ang), jnp.cos(ang)], axis=1), (1, n_heads))
    sin = jnp.tile(jnp.concatenate([-jnp.sin(ang), jnp.sin(ang)], axis=1), (1, n_heads))
    return cos, sin


def kernel(x_prompt, x_sample, cache_k, cache_v, cache_idx_k, state_conv, page_table, g_ffn1, w_ffn1_in, w_ffn1_out, g_mix, w_in, g_q, g_k, w_dw, b_dw, g_cln, b_cln, w_cout, w_aout, w_o, g_ffn2, w_ffn2_in, w_ffn2_out):
    b, t, d = x_prompt.shape
    db, ts, _ = x_sample.shape
    depth = g_mix.shape[0]
    n_pages = page_table.shape[1]
    n_phys = cache_k.shape[1]
    past_len = n_pages * PAGE_SIZE
    hist_rows = CONV_W - 1

    pos_p = jnp.arange(t)
    pos_s = jnp.tile(past_len + jnp.arange(ts), db)
    tab_p = _rope_tables(pos_p, HEAD_DIM, N_HEADS) + _rope_tables(pos_p, IDX_DIM, N_IDX_HEADS)
    tab_s = _rope_tables(pos_s, HEAD_DIM, N_HEADS) + _rope_tables(pos_s, IDX_DIM, N_IDX_HEADS)
    seg = jnp.kron(jnp.eye(MXU_TILE // HEAD_DIM, dtype=F32),
                   jnp.full((HEAD_DIM, HEAD_DIM), 1.0 / HEAD_DIM, F32)).astype(BF16)

    xp = x_prompt.reshape(b * t, d)
    xs = x_sample.reshape(db * ts, d)
    outs = [[] for _ in range(8)]
    for l in range(depth):
        vec = lambda a: a[l][None, :]
        w1i, w1o = w_ffn1_in[l].astype(BF16), w_ffn1_out[l].astype(BF16)
        w2i, w2o = w_ffn2_in[l].astype(BF16), w_ffn2_out[l].astype(BF16)
        c_main = 2 * D_CONV + 3 * D_ATTN + D_IDX
        w_ki = w_in[l][:, c_main:c_main + IDX_DIM]
        w_wi = w_in[l][:, c_main + IDX_DIM:c_main + IDX_DIM + N_IDX_HEADS]
        w_proj = jnp.concatenate([w_in[l][:, :c_main], jnp.tile(w_ki, (1, N_IDX_HEADS)),
                                  jnp.pad(w_wi, ((0, 0), (0, LANES - N_IDX_HEADS)))], axis=1).astype(BF16)
        w_gates = w_in[l][:, c_main + IDX_DIM + N_IDX_HEADS:].astype(BF16)
        gq = jnp.tile(g_q[l], N_HEADS)[None, :]
        gk = jnp.tile(g_k[l], N_HEADS)[None, :]
        wc, wa, wo = w_cout[l].astype(BF16), w_aout[l].astype(BF16), w_o[l].astype(BF16)

        xp = _ffn(xp, vec(g_ffn1), w1i, w1o)
        xs = _ffn(xs, vec(g_ffn1), w1i, w1o)
        up, kp, vp, kip, qbp, kbp, vbp, qibp, ki8bp, wip = _proj(xp, vec(g_mix), w_proj, gq, gk, seg, tab_p, t, True)
        us, ks, vs, kis, qbs, kbs, vbs, qibs, ki8bs, wis = _proj(xs, vec(g_mix), w_proj, gq, gk, seg, tab_s, db * ts,
                                                                 False)

        up3, us3 = up.reshape(b, t, D_CONV), us.reshape(db, ts, D_CONV)
        lead = ((0, 0), (HALO - hist_rows, 0), (0, 0))
        hist_p = jnp.zeros((b, HALO, D_CONV), F32)
        hist_s = jnp.pad(state_conv[l], lead)
        conv_args = (w_dw[l], vec(b_dw), vec(g_cln), vec(b_cln))
        cyp = _conv(up3, hist_p, *conv_args)
        cys = _conv(us3, hist_s, *conv_args)
        st_p = jnp.concatenate([jnp.zeros((b, hist_rows, D_CONV), F32), up3], axis=1)[:, -hist_rows:]
        st_s = jnp.concatenate([state_conv[l], us3], axis=1)[:, -hist_rows:]

        r3 = lambda a, n, w: a.reshape(n, -1, w)
        att_p = _attn_prompt(r3(qbp, b, D_ATTN), r3(kbp, b, D_ATTN), r3(vbp, b, D_ATTN),
                             r3(qibp, b, D_IDX), r3(ki8bp, b, D_IDX), r3(wip, b, LANES))
        head_major = lambda a, nh, w: a.reshape(db, ts, nh, w).transpose(0, 2, 1, 3).reshape(db, nh * ts, w)
        new_page_t = lambda a: jnp.pad(a.reshape(db, ts, -1).transpose(0, 2, 1), ((0, 0), (0, 0), (0, PAGE_SIZE - ts)))
        att_s = _attn_sample(
            l, page_table, r3(qbs, db, D_ATTN), new_page_t(kbs), new_page_t(vbs),
            head_major(qibs, N_IDX_HEADS, IDX_DIM), new_page_t(ki8bs[:, :IDX_DIM]),
            head_major(wis[:, :N_IDX_HEADS], N_IDX_HEADS, 1),
            cache_k.transpose(0, 1, 3, 4, 2).reshape(depth, n_phys, D_ATTN, PAGE_SIZE),
            cache_v.transpose(0, 1, 3, 4, 2).reshape(depth, n_phys, D_ATTN, PAGE_SIZE),
            cache_idx_k.transpose(0, 1, 3, 2))

        xp = _merge(xp, cyp.reshape(b * t, D_CONV), att_p.reshape(b * t, D_ATTN), vec(g_mix), w_gates, wc, wa, wo)
        xs = _merge(xs, cys.reshape(db * ts, D_CONV), att_s.reshape(db * ts, D_ATTN), vec(g_mix), w_gates, wc, wa, wo)
        xp = _ffn(xp, vec(g_ffn2), w2i, w2o)
        xs = _ffn(xs, vec(g_ffn2), w2i, w2o)

        token_major = lambda a: a.reshape(b, N_HEADS, HEAD_DIM, t).transpose(0, 3, 1, 2)
        for lst, val in zip(outs, (token_major(kp), token_major(vp), kip.transpose(0, 2, 1), st_p,
                                   ks.reshape(db, ts, N_HEADS, HEAD_DIM), vs.reshape(db, ts, N_HEADS, HEAD_DIM),
                                   kis.reshape(db, ts, IDX_DIM), st_s)):
            lst.append(val)

    return (xp.reshape(b, t, d), xs.reshape(db, ts, d)) + tuple(jnp.stack(o) for o in outs)
```

```python
import functools
import math

import jax
import jax.numpy as jnp
from jax import lax
from jax.experimental import pallas as pl
from jax.experimental.pallas import tpu as pltpu

N_HEADS = 8
HEAD_DIM = 64
D_ATTN = N_HEADS * HEAD_DIM
D_CONV = 512
CONV_W = 31
N_IDX_HEADS = 8
IDX_DIM = 32
D_IDX = N_IDX_HEADS * IDX_DIM
TOPK_MAX = 256
PAGE_SIZE = 128
ROPE_THETA = 10000.0
EPS = 1e-6
IDX_W_SCALE = D_IDX ** -0.5

LANES = 128
SUBLANES = 8
MXU_TILE = 256
LOG2E = math.log2(math.e)
HALO = 32
INT_MIN = -2 ** 31
NEG_INF_KEY = INT_MIN + 0x7FFFFF
MASKED_LOGIT = -1e30
SOFTMAX_FLOOR = 0.25 * MASKED_LOGIT
COUNT_FOLD = 16 * LANES
VMEM_LIMIT = 56 * 1024 * 1024

F32 = jnp.float32
BF16 = jnp.bfloat16
I32 = jnp.int32
NT_DIMS = (((1,), (1,)), ((), ()))


def _row_tile(n, cap):
    t = cap
    while n % t:
        t //= 2
    return t


def _rms(x, g):
    return x * lax.rsqrt(jnp.mean(x * x, axis=-1, keepdims=True) + EPS) * g


def _const_spec(shape):
    nd = len(shape)
    return pl.BlockSpec(shape, lambda *_: (0,) * nd, pipeline_mode=pl.Buffered(1))


def _ffn_body(x_ref, g_ref, win_ref, wout_ref, o_ref, *, d_ff, n_chunks):
    x = x_ref[...]
    h = _rms(x, g_ref[...]).astype(BF16)
    fc = d_ff // n_chunks
    acc = jnp.zeros_like(x)
    for c in range(n_chunks):
        gate = jnp.dot(h, win_ref[:, c * fc:(c + 1) * fc], preferred_element_type=F32)
        up = jnp.dot(h, win_ref[:, d_ff + c * fc:d_ff + (c + 1) * fc], preferred_element_type=F32)
        act = (gate * jax.nn.sigmoid(gate) * up).astype(BF16)
        acc = acc + jnp.dot(act, wout_ref[c * fc:(c + 1) * fc, :], preferred_element_type=F32)
    o_ref[...] = x + 0.5 * acc


def _ffn(x, g, w_in_bf, w_out_bf):
    n, d = x.shape
    d_ff = w_out_bf.shape[0]
    tm = _row_tile(n, 512)
    n_chunks = 2 if d_ff % (2 * LANES) == 0 else 1
    return pl.pallas_call(
        functools.partial(_ffn_body, d_ff=d_ff, n_chunks=n_chunks),
        grid=(n // tm,),
        in_specs=[pl.BlockSpec((tm, d), lambda i: (i, 0)),
                  _const_spec((1, d)), _const_spec((d, 2 * d_ff)), _const_spec((d_ff, d))],
        out_specs=pl.BlockSpec((tm, d), lambda i: (i, 0)),
        out_shape=jax.ShapeDtypeStruct((n, d), F32),
        compiler_params=pltpu.CompilerParams(dimension_semantics=("parallel",), vmem_limit_bytes=VMEM_LIMIT),
        name="ffn",
    )(x, g, w_in_bf, w_out_bf)


def _rope(x, cos, sin_signed, half):
    n = x.shape[-1]
    lane = lax.broadcasted_iota(I32, x.shape, 1)
    first = (lane % (2 * half)) < half
    partner = jnp.where(first, pltpu.roll(x, n - half, 1), pltpu.roll(x, half, 1))
    return x * cos + partner * sin_signed


def _head_rms(x, seg_ref, g):
    x2 = x * x
    hi = x2.astype(BF16)
    lo = (x2 - hi.astype(F32)).astype(BF16)
    w = seg_ref.shape[0]
    ms = jnp.concatenate([jnp.dot(hi[:, c:c + w], seg_ref[...], preferred_element_type=F32)
                          + jnp.dot(lo[:, c:c + w], seg_ref[...], preferred_element_type=F32)
                          for c in range(0, x.shape[1], w)], axis=1)
    return x * lax.rsqrt(ms + EPS) * g


def _proj_body(x_ref, g_ref, w_ref, gq_ref, gk_ref, seg_ref, cqk_ref, sqk_ref, ci_ref, si_ref,
               u_ref, k_ref, v_ref, ki_ref, qb_ref, kb_ref, vb_ref, qib_ref, ki8b_ref, wi_ref, *, feature_major):
    h = _rms(x_ref[...], g_ref[...]).astype(BF16)
    p = jnp.dot(h, w_ref[...], preferred_element_type=F32)
    c0 = 0
    a_glu = p[:, c0:c0 + D_CONV]; c0 += D_CONV
    b_glu = p[:, c0:c0 + D_CONV]; c0 += D_CONV
    q = p[:, c0:c0 + D_ATTN]; c0 += D_ATTN
    k = p[:, c0:c0 + D_ATTN]; c0 += D_ATTN
    v = p[:, c0:c0 + D_ATTN]; c0 += D_ATTN
    qi = p[:, c0:c0 + D_IDX]; c0 += D_IDX
    ki8 = p[:, c0:c0 + D_IDX]; c0 += D_IDX
    wi = p[:, c0:c0 + LANES]

    u_ref[...] = a_glu * jax.nn.sigmoid(b_glu)
    q = _rope(_head_rms(q, seg_ref, gq_ref[...]), cqk_ref[...], sqk_ref[...], HEAD_DIM // 2)
    k = _rope(_head_rms(k, seg_ref, gk_ref[...]), cqk_ref[...], sqk_ref[...], HEAD_DIM // 2)
    qi = _rope(qi, ci_ref[...], si_ref[...], IDX_DIM // 2)
    ki8 = _rope(ki8, ci_ref[...], si_ref[...], IDX_DIM // 2)

    if feature_major:
        k_ref[0] = k.T
        v_ref[0] = v.T
        ki_ref[0] = ki8.T[:IDX_DIM, :]
    else:
        k_ref[...] = k
        v_ref[...] = v
        ki_ref[...] = ki8[:, :IDX_DIM]
    qb_ref[...] = (q * (HEAD_DIM ** -0.5 * LOG2E)).astype(BF16)
    kb_ref[...] = k.astype(BF16)
    vb_ref[...] = v.astype(BF16)
    qib_ref[...] = qi.astype(BF16)
    ki8b_ref[...] = ki8.astype(BF16)
    wi_ref[...] = wi * IDX_W_SCALE


def _proj(x, g, w_bf, gq, gk, seg, tables, tab_rows, feature_major):
    n, d = x.shape
    tm = _row_tile(math.gcd(n, tab_rows), 512)
    tab_blocks = tab_rows // tm
    row = lambda w: pl.BlockSpec((tm, w), lambda i: (i, 0))
    tab = lambda w: pl.BlockSpec((tm, w), lambda i: (i % tab_blocks, 0))
    f32o = lambda w: jax.ShapeDtypeStruct((n, w), F32)
    bfo = lambda w: jax.ShapeDtypeStruct((n, w), BF16)
    if feature_major:
        cache = lambda w: pl.BlockSpec((1, w, tm), lambda i: (i // tab_blocks, 0, i % tab_blocks))
        cacheo = lambda w: jax.ShapeDtypeStruct((n // tab_rows, w, tab_rows), F32)
    else:
        cache, cacheo = row, f32o
    return pl.pallas_call(
        functools.partial(_proj_body, feature_major=feature_major),
        grid=(n // tm,),
        in_specs=[row(d), _const_spec((1, d)), _const_spec(w_bf.shape),
                  _const_spec((1, D_ATTN)), _const_spec((1, D_ATTN)), _const_spec(seg.shape),
                  tab(D_ATTN), tab(D_ATTN), tab(D_IDX), tab(D_IDX)],
        out_specs=[row(D_CONV), cache(D_ATTN), cache(D_ATTN), cache(IDX_DIM),
                   row(D_ATTN), row(D_ATTN), row(D_ATTN), row(D_IDX), row(D_IDX), row(LANES)],
        out_shape=[f32o(D_CONV), cacheo(D_ATTN), cacheo(D_ATTN), cacheo(IDX_DIM),
                   bfo(D_ATTN), bfo(D_ATTN), bfo(D_ATTN), bfo(D_IDX), bfo(D_IDX), f32o(LANES)],
        compiler_params=pltpu.CompilerParams(dimension_semantics=("parallel",), vmem_limit_bytes=VMEM_LIMIT),
        name="proj",
    )(x, g, w_bf, gq, gk, seg, *tables)


def _conv_body(u_ref, uprev_ref, hist_ref, w_ref, b_ref, g_ref, bl_ref, o_ref, xs_ref, sh_ref, *, tc, n_t):
    xs_ref[HALO:HALO + tc, :] = u_ref[0]
    xs_ref[HALO + tc:HALO + tc + SUBLANES, :] = jnp.zeros((SUBLANES, D_CONV), F32)
    if n_t == 1:
        xs_ref[0:HALO, :] = hist_ref[0]
    else:
        i = pl.program_id(1)

        @pl.when(i == 0)
        def _():
            xs_ref[0:HALO, :] = hist_ref[0]

        @pl.when(i > 0)
        def _():
            xs_ref[0:HALO, :] = uprev_ref[0, tc - HALO:tc, :]

    n_sh = sh_ref.shape[1]
    for s in range(SUBLANES):
        sh_ref[s] = xs_ref[s:s + n_sh, :]

    rb = min(tc, 32)
    lead = HALO - (CONV_W - 1)
    for r0 in range(0, tc, rb):
        acc = jnp.broadcast_to(b_ref[...], (rb, D_CONV))
        for j in range(CONV_W):
            s = (lead + j) % SUBLANES
            base = r0 + lead + j - s
            acc = acc + w_ref[j:j + 1, :] * sh_ref[s, base:base + rb, :]
        mu = jnp.mean(acc, axis=-1, keepdims=True)
        yc = acc - mu
        y = yc * lax.rsqrt(jnp.mean(yc * yc, axis=-1, keepdims=True) + EPS) * g_ref[...] + bl_ref[...]
        o_ref[0, r0:r0 + rb, :] = y * jax.nn.sigmoid(y)


def _conv(u, hist, w_dw, b_dw, g_ln, b_ln):
    b, t, c = u.shape
    tc = _row_tile(t, 256)
    n_t = t // tc
    assert n_t == 1 or tc >= HALO
    vec = _const_spec((1, c))
    return pl.pallas_call(
        functools.partial(_conv_body, tc=tc, n_t=n_t),
        grid=(b, n_t),
        in_specs=[pl.BlockSpec((1, tc, c), lambda bi, i: (bi, i, 0)),
                  pl.BlockSpec((1, tc, c), lambda bi, i: (bi, jnp.maximum(i - 1, 0), 0)),
                  pl.BlockSpec((1, HALO, c), lambda bi, i: (bi, 0, 0)),
                  _const_spec((CONV_W, c)), vec, vec, vec],
        out_specs=pl.BlockSpec((1, tc, c), lambda bi, i: (bi, i, 0)),
        out_shape=jax.ShapeDtypeStruct((b, t, c), F32),
        scratch_shapes=[pltpu.VMEM((HALO + tc + SUBLANES, c), F32),
                        pltpu.VMEM((SUBLANES, HALO + tc, c), F32)],
        compiler_params=pltpu.CompilerParams(dimension_semantics=("parallel", "parallel")),
        name="conv",
    )(u, u, hist, w_dw, b_dw, g_ln, b_ln)


def _rank_to_f32(u):
    key = u + NEG_INF_KEY
    return lax.bitcast_convert_type(key ^ ((key >> 31) & 0x7FFFFFFF), F32)


def _count(mask):
    ones = jnp.where(mask, 1.0, 0.0)
    n = ones.shape[1]
    if n > COUNT_FOLD:
        parts = [ones[:, i:i + COUNT_FOLD] for i in range(0, n - n % COUNT_FOLD, COUNT_FOLD)]
        while len(parts) > 1:
            parts = [a + b for a, b in zip(parts[0::2], parts[1::2])] + parts[len(parts) & ~1:]
        total = jnp.sum(parts[0], axis=1, keepdims=True)
        if n % COUNT_FOLD:
            total = total + jnp.sum(ones[:, n - n % COUNT_FOLD:], axis=1, keepdims=True)
        return total
    return jnp.sum(ones, axis=1, keepdims=True)


def _select_topk_bias(sc_ref, code_ref, admissible, col, topk, n_idx_bits, bits_per_step):
    rows = sc_ref.shape[0]
    n_cand = 2 ** bits_per_step - 1
    assert 32 % bits_per_step == 0
    n_idx_bits = -(-n_idx_bits // bits_per_step) * bits_per_step

    def value_step(i, u):
        shift = 32 - bits_per_step * (i + 1)
        unit = jnp.int32(INT_MIN if shift == 31 else 1 << shift)
        sc = sc_ref[...]
        taken = jnp.zeros((rows, 1), I32)
        for j in range(1, n_cand + 1):
            taken = taken + jnp.where(_count(sc >= _rank_to_f32(u + j * unit)) >= topk, 1, 0)
        return u + taken * unit

    u = jnp.zeros((rows, 1), I32)
    for i in range(32 // bits_per_step):
        u = value_step(i, u)
    thr = _rank_to_f32(u)

    sc = sc_ref[...]
    code_ref[...] = jnp.where(sc > thr, -1, jnp.where(sc == thr, col, 2 ** n_idx_bits))

    def index_step(i, bound):
        unit = 1 << (n_idx_bits - bits_per_step * (i + 1))
        code = code_ref[...]
        taken = jnp.zeros((rows, 1), I32)
        for j in range(1, n_cand + 1):
            taken = taken + jnp.where(_count(code < bound + j * unit) < topk, 1, 0)
        return bound + taken * unit

    bound = jnp.zeros((rows, 1), I32)
    for i in range(n_idx_bits // bits_per_step):
        bound = index_step(i, bound)
    sc_ref[...] = jnp.where(admissible & (code_ref[...] <= bound), 0.0, MASKED_LOGIT)


def _pattern_to_f32(u, bf16_exact=False):
    key = jnp.maximum(u ^ INT_MIN, NEG_INF_KEY)
    bits = key ^ ((key >> 31) & 0x7FFFFFFF)
    if bf16_exact:
        bits = bits & -(1 << 16)
    return lax.bitcast_convert_type(bits, F32)


def _f32_to_pattern(x):
    bits = lax.bitcast_convert_type(x, I32)
    return (bits ^ ((bits >> 31) & 0x7FFFFFFF)) ^ INT_MIN


def _select_topk_bias_dense(sc_ref, sb_ref, tri_ref, topk):
    rows, n = sc_ref.shape

    def unrolled(n_steps, body, carry):
        for i in range(n_steps):
            carry = body(i, carry)
        return carry
    one, zero = jnp.ones((), BF16), jnp.zeros((), BF16)
    sb_ref[...] = sc_ref[...].astype(BF16)

    def count_bf16(cand):
        cb = jnp.broadcast_to(cand, (rows, LANES)).astype(BF16)
        acc = jnp.where(sb_ref[:, 0:LANES] >= cb, one, zero)
        for j in range(1, n // LANES):
            acc = acc + jnp.where(sb_ref[:, j * LANES:(j + 1) * LANES] >= cb, one, zero)
        return jnp.sum(acc.astype(F32), axis=1, keepdims=True)

    def coarse_step(i, p):
        cand = p + (1 << (15 - i))
        return jnp.where(count_bf16(_pattern_to_f32(cand << 16, bf16_exact=True)) >= topk, cand, p)

    p = unrolled(16, coarse_step, jnp.zeros((rows, 1), I32))
    base = _f32_to_pattern(_pattern_to_f32(p << 16, bf16_exact=True)) - (1 << 15)

    def fine_step(i, d):
        cand = d + (1 << (16 - i))
        return jnp.where(_count(sc_ref[...] >= _pattern_to_f32(base + cand)) >= topk, cand, d)

    thr = _pattern_to_f32(base + unrolled(17, fine_step, jnp.zeros((rows, 1), I32)))

    need = jnp.where(thr == -jnp.inf, 0.0, topk - _count(sc_ref[...] > thr))
    cw = tri_ref.shape[0]
    seen = jnp.zeros((rows, 1), F32)
    for c in range(n // cw):
        s = sc_ref[:, c * cw:(c + 1) * cw]
        tie = jnp.where(s == thr, 1.0, 0.0)
        earlier = jnp.dot(tie.astype(BF16), tri_ref[...], preferred_element_type=F32) + seen
        keep_tie = jnp.where(earlier <= need, 0.0, MASKED_LOGIT)
        sc_ref[:, c * cw:(c + 1) * cw] = jnp.where(s > thr, 0.0, jnp.where(s == thr, keep_tie, MASKED_LOGIT))
        seen = seen + jnp.sum(tie, axis=1, keepdims=True)


def _attn_prompt_body(q_ref, k_ref, v_ref, qi_ref, ki8_ref, wi_ref, tri_ref, o_ref, bias_ref, sb_ref,
                      *, tq, q0, topk):
    t = q0 + tq
    qi = qi_ref[0]
    ki8 = ki8_ref[0]
    wi = wi_ref[0]
    idx_head = lax.broadcasted_iota(I32, (1, D_IDX), 1) // IDX_DIM
    scores = jnp.zeros((tq, t), F32)
    for h in range(N_IDX_HEADS):
        qh = qi * jnp.where(idx_head == h, 1.0, 0.0).astype(BF16)
        dots = lax.dot_general(qh, ki8, NT_DIMS, preferred_element_type=F32)
        scores = scores + wi[:, h:h + 1] * jnp.maximum(dots, 0.0)

    col = lax.broadcasted_iota(I32, (tq, t), 1)
    qpos = q0 + lax.broadcasted_iota(I32, (tq, t), 0)
    bias_ref[...] = jnp.where(col <= qpos, scores, -jnp.inf)
    _select_topk_bias_dense(bias_ref, sb_ref, tri_ref, topk)

    pair_head = lax.broadcasted_iota(I32, (1, LANES), 1) // HEAD_DIM
    for j in range(D_ATTN // LANES):
        sl = slice(j * LANES, (j + 1) * LANES)
        qj = q_ref[0, :, sl]
        kj = k_ref[0, :, sl]
        vj = jnp.concatenate([v_ref[0, :, sl], jnp.ones((t, LANES), BF16)], axis=1)
        out = jnp.zeros((tq, LANES), F32)
        for half in range(LANES // HEAD_DIM):
            mine = pair_head == half
            qh = qj * jnp.where(mine, 1.0, 0.0).astype(BF16)
            logits = lax.dot_general(qh, kj, NT_DIMS, preferred_element_type=F32) + bias_ref[...]
            m = jnp.max(logits, axis=1, keepdims=True)
            p = jnp.exp2(logits - m)
            pv = jnp.dot(p.astype(BF16), vj, preferred_element_type=F32)
            out = jnp.where(mine, pv[:, :LANES] / pv[:, LANES:], out)
        o_ref[0, :, sl] = out


def _attn_prompt(q_bf, k_bf, v_bf, qi_bf, ki8_bf, wi):
    b, t, _ = q_bf.shape
    tq = _row_tile(t, 256)
    topk = min(TOPK_MAX, t // 4)
    cw = min(tq, 2 * LANES)
    assert tq % cw == 0 and cw % LANES == 0
    tri = jnp.triu(jnp.ones((cw, cw), BF16))
    blocks = []
    for qb in range(t // tq):
        kv = (qb + 1) * tq
        qspec = lambda w, qb=qb: pl.BlockSpec((1, tq, w), lambda bi: (bi, qb, 0))
        kspec = lambda w, kv=kv: pl.BlockSpec((1, kv, w), lambda bi: (bi, 0, 0))
        blocks.append(pl.pallas_call(
            functools.partial(_attn_prompt_body, tq=tq, q0=qb * tq, topk=topk),
            grid=(b,),
            in_specs=[qspec(D_ATTN), kspec(D_ATTN), kspec(D_ATTN), qspec(D_IDX), kspec(D_IDX), qspec(LANES),
                      pl.BlockSpec((cw, cw), lambda bi: (0, 0))],
            out_specs=pl.BlockSpec((1, tq, D_ATTN), lambda bi: (bi, 0, 0)),
            out_shape=jax.ShapeDtypeStruct((b, tq, D_ATTN), F32),
            scratch_shapes=[pltpu.VMEM((tq, kv), F32), pltpu.VMEM((tq, kv), BF16)],
            compiler_params=pltpu.CompilerParams(dimension_semantics=("parallel",), vmem_limit_bytes=VMEM_LIMIT),
            name=f"attn_prompt_q{qb}",
        )(q_bf, k_bf, v_bf, qi_bf, ki8_bf, wi, tri))
    return jnp.concatenate(blocks, axis=1)


def _attn_sample_body(pt_ref, q_ref, knt_ref, vnt_ref, qi_ref, kint_ref, wrow_ref, ck_hbm, cv_hbm, cik_hbm,
                      o_ref, ikbuf, kbuf, vbuf, bias_ref, code_ref, sem_ik, sem_k, sem_v,
                      *, layer, n_seq, n_pages, ts, pc_ik, pc_kv, kv_slots, topk):
    seq = pl.program_id(0)
    n_past = n_pages * PAGE_SIZE
    s_pad = n_past + PAGE_SIZE
    rows = N_HEADS * ts
    n_ik_chunks = n_pages // pc_ik
    n_kv_chunks = n_pages // pc_kv
    n_kv_total = n_seq * n_kv_chunks

    def page_copy(hbm, buf, sem, s, page, slot, i):
        return pltpu.make_async_copy(hbm.at[layer, pt_ref[s, page]], buf.at[slot, i], sem.at[slot])

    def ik_copies(s):
        return [page_copy(cik_hbm, ikbuf, sem_ik, s, p, s % 2, p) for p in range(n_pages)]

    def kv_copies(g):
        s, c = g // n_kv_chunks, g % n_kv_chunks
        slot = g % kv_slots
        return [page_copy(hbm, buf, sem, s, c * pc_kv + i, slot, i)
                for hbm, buf, sem in ((ck_hbm, kbuf, sem_k), (cv_hbm, vbuf, sem_v)) for i in range(pc_kv)]

    @pl.when(seq == 0)
    def _():
        for cp in ik_copies(0):
            cp.start()
        for g in range(min(kv_slots - 1, n_kv_total)):
            for cp in kv_copies(g):
                cp.start()

    @pl.when(seq + 1 < n_seq)
    def _():
        for cp in ik_copies(seq + 1):
            cp.start()

    qi = qi_ref[0]
    wrow = wrow_ref[0]

    def page_scores(keys_t_bf):
        r = jnp.maximum(jnp.dot(qi, keys_t_bf, preferred_element_type=F32), 0.0) * wrow
        s = r[0:ts]
        for h in range(1, N_IDX_HEADS):
            s = s + r[h * ts:(h + 1) * ts]
        return s

    for cp in ik_copies(seq):
        cp.wait()

    def ik_chunk(c, carry):
        for i in range(pc_ik):
            page = c * pc_ik + i
            off = pl.multiple_of(page * PAGE_SIZE, PAGE_SIZE)
            bias_ref[:, pl.ds(off, PAGE_SIZE)] = page_scores(ikbuf[seq % 2, page].astype(BF16))
        return carry

    lax.fori_loop(0, n_ik_chunks, ik_chunk, 0)
    bias_ref[:, n_past:s_pad] = page_scores(kint_ref[0])

    col = lax.broadcasted_iota(I32, (ts, s_pad), 1)
    tpos = lax.broadcasted_iota(I32, (ts, s_pad), 0)
    admissible = (col - n_past) <= tpos
    bias_ref[...] = jnp.where(admissible, bias_ref[...], -jnp.inf)
    _select_topk_bias(bias_ref, code_ref, admissible, col, topk, max(1, (s_pad - 1).bit_length()), bits_per_step=2)

    qf = q_ref[0].astype(F32)
    row_head = lax.broadcasted_iota(I32, (rows, D_ATTN), 0) // ts
    lane_head = lax.broadcasted_iota(I32, (rows, D_ATTN), 1) // HEAD_DIM
    qbd = jnp.where(row_head == lane_head, jnp.concatenate([qf] * N_HEADS, axis=0), 0.0).astype(BF16)

    def attend(carry, pages_k, pages_v, bias):
        m, l, acc = carry
        n = len(pages_k) * PAGE_SIZE
        logits = jnp.concatenate([jnp.dot(qbd, kt, preferred_element_type=F32) for kt in pages_k], axis=1)
        logits = (logits.reshape(N_HEADS, ts, n) + bias[None]).reshape(rows, n)
        m_new = jnp.maximum(m, jnp.max(logits, axis=1, keepdims=True))
        alpha = jnp.exp2(m - m_new)
        p = jnp.exp2(logits - m_new)
        l = alpha * l + jnp.sum(p, axis=1, keepdims=True)
        p = p.astype(BF16)
        acc = alpha * acc
        for i, vt in enumerate(pages_v):
            acc = acc + lax.dot_general(p[:, i * PAGE_SIZE:(i + 1) * PAGE_SIZE], vt, NT_DIMS,
                                        preferred_element_type=F32)
        return m_new, l, acc

    n_chunk_keys = pc_kv * PAGE_SIZE

    def kv_chunk(c, carry):
        g = seq * n_kv_chunks + c
        slot = g % kv_slots

        @pl.when(g + kv_slots - 1 < n_kv_total)
        def _():
            for cp in kv_copies(g + kv_slots - 1):
                cp.start()

        for cp in kv_copies(g):
            cp.wait()
        off = pl.multiple_of(c * n_chunk_keys, n_chunk_keys)
        return attend(carry, [kbuf[slot, i].astype(BF16) for i in range(pc_kv)],
                      [vbuf[slot, i].astype(BF16) for i in range(pc_kv)], bias_ref[:, pl.ds(off, n_chunk_keys)])

    init = (jnp.full((rows, 1), SOFTMAX_FLOOR, F32), jnp.zeros((rows, 1), F32), jnp.zeros((rows, D_ATTN), F32))
    carry = lax.fori_loop(0, n_kv_chunks, kv_chunk, init)
    _, l, acc = attend(carry, [knt_ref[0]], [vnt_ref[0]], bias_ref[:, n_past:s_pad])

    acc = acc / l
    out = jnp.zeros((ts, D_ATTN), F32)
    head_of_lane = lax.broadcasted_iota(I32, (ts, D_ATTN), 1) // HEAD_DIM
    for h in range(N_HEADS):
        out = jnp.where(head_of_lane == h, acc[h * ts:(h + 1) * ts], out)
    o_ref[0] = out


def _attn_sample(layer, page_table, q_bf, knt_bf, vnt_bf, qi_rows, kint_bf, wrow, cache_kt, cache_vt, cache_ikt):
    db, ts, _ = q_bf.shape
    n_pages = page_table.shape[1]
    assert ts == SUBLANES, "one query block of 8 sublanes per sequence"
    pc_ik = _row_tile(n_pages, 32)
    pc_kv = _row_tile(n_pages, 16)
    kv_slots = 5
    s_pad = (n_pages + 1) * PAGE_SIZE
    topk = min(TOPK_MAX, (n_pages * PAGE_SIZE + ts) // 4)
    rows = N_HEADS * ts
    per_seq = lambda r, w: pl.BlockSpec((1, r, w), lambda s, pt: (s, 0, 0))
    any_spec = pl.BlockSpec(memory_space=pl.ANY)
    grid_spec = pltpu.PrefetchScalarGridSpec(
        num_scalar_prefetch=1,
        grid=(db,),
        in_specs=[per_seq(ts, D_ATTN), per_seq(D_ATTN, PAGE_SIZE), per_seq(D_ATTN, PAGE_SIZE),
                  per_seq(rows, IDX_DIM), per_seq(IDX_DIM, PAGE_SIZE), per_seq(rows, 1),
                  any_spec, any_spec, any_spec],
        out_specs=per_seq(ts, D_ATTN),
        scratch_shapes=[pltpu.VMEM((2, n_pages, IDX_DIM, PAGE_SIZE), F32),
                        pltpu.VMEM((kv_slots, pc_kv, D_ATTN, PAGE_SIZE), F32),
                        pltpu.VMEM((kv_slots, pc_kv, D_ATTN, PAGE_SIZE), F32),
                        pltpu.VMEM((ts, s_pad), F32), pltpu.VMEM((ts, s_pad), I32),
                        pltpu.SemaphoreType.DMA((2,)), pltpu.SemaphoreType.DMA((kv_slots,)),
                        pltpu.SemaphoreType.DMA((kv_slots,))])
    return pl.pallas_call(
        functools.partial(_attn_sample_body, layer=layer, n_seq=db, n_pages=n_pages, ts=ts, pc_ik=pc_ik,
                          pc_kv=pc_kv, kv_slots=kv_slots, topk=topk),
        grid_spec=grid_spec,
        out_shape=jax.ShapeDtypeStruct((db, ts, D_ATTN), F32),
        compiler_params=pltpu.CompilerParams(dimension_semantics=("arbitrary",), vmem_limit_bytes=VMEM_LIMIT),
        name="attn_sample",
    )(page_table, q_bf, knt_bf, vnt_bf, qi_rows, kint_bf, wrow, cache_kt, cache_vt, cache_ikt)


def _merge_body(x_ref, cy_ref, att_ref, g_ref, wg_ref, wc_ref, wa_ref, wo_ref, o_ref, *, d):
    x = x_ref[...]
    h = _rms(x, g_ref[...]).astype(BF16)
    gates = jax.nn.sigmoid(jnp.dot(h, wg_ref[...], preferred_element_type=F32))
    conv_out = jnp.dot(cy_ref[...].astype(BF16), wc_ref[...], preferred_element_type=F32)
    a = jnp.dot(att_ref[...].astype(BF16), wa_ref[...], preferred_element_type=F32)
    mixed = (gates[:, :d] * conv_out + gates[:, d:] * a).astype(BF16)
    o_ref[...] = x + jnp.dot(mixed, wo_ref[...], preferred_element_type=F32)


def _merge(x, conv_y, att, g, wg_bf, wc_bf, wa_bf, wo_bf):
    n, d = x.shape
    tm = _row_tile(n, 512)
    row = lambda w: pl.BlockSpec((tm, w), lambda i: (i, 0))
    return pl.pallas_call(
        functools.partial(_merge_body, d=d),
        grid=(n // tm,),
        in_specs=[row(d), row(D_CONV), row(D_ATTN), _const_spec((1, d)), _const_spec(wg_bf.shape),
                  _const_spec(wc_bf.shape), _const_spec(wa_bf.shape), _const_spec(wo_bf.shape)],
        out_specs=row(d),
        out_shape=jax.ShapeDtypeStruct((n, d), F32),
        compiler_params=pltpu.CompilerParams(dimension_semantics=("parallel",), vmem_limit_bytes=VMEM_LIMIT),
        name="merge",
    )(x, conv_y, att, g, wg_bf, wc_bf, wa_bf, wo_bf)


def _rope_tables(pos, head_dim, n_heads):
    half = head_dim // 2
    inv = jnp.float32(ROPE_THETA) ** (-jnp.arange(half, dtype=F32) * (2.0 / head_dim))
    ang = pos.astype(F32)[:, None] * inv[None, :]
    cos = jnp.tile(jnp.concatenate([jnp.cos(ang), jnp.cos(ang)], axis=1), (1, n_heads))
    sin = jnp.tile(jnp.concatenate([-jnp.sin(ang), jnp.sin(ang)], axis=1), (1, n_heads))
    return cos, sin


def kernel(x_prompt, x_sample, cache_k, cache_v, cache_idx_k, state_conv, page_table, g_ffn1, w_ffn1_in, w_ffn1_out, g_mix, w_in, g_q, g_k, w_dw, b_dw, g_cln, b_cln, w_cout, w_aout, w_o, g_ffn2, w_ffn2_in, w_ffn2_out):
    b, t, d = x_prompt.shape
    db, ts, _ = x_sample.shape
    depth = g_mix.shape[0]
    n_pages = page_table.shape[1]
    n_phys = cache_k.shape[1]
    past_len = n_pages * PAGE_SIZE
    hist_rows = CONV_W - 1

    pos_p = jnp.arange(t)
    pos_s = jnp.tile(past_len + jnp.arange(ts), db)
    tab_p = _rope_tables(pos_p, HEAD_DIM, N_HEADS) + _rope_tables(pos_p, IDX_DIM, N_IDX_HEADS)
    tab_s = _rope_tables(pos_s, HEAD_DIM, N_HEADS) + _rope_tables(pos_s, IDX_DIM, N_IDX_HEADS)
    seg = jnp.kron(jnp.eye(MXU_TILE // HEAD_DIM, dtype=F32),
                   jnp.full((HEAD_DIM, HEAD_DIM), 1.0 / HEAD_DIM, F32)).astype(BF16)

    xp = x_prompt.reshape(b * t, d)
    xs = x_sample.reshape(db * ts, d)
    outs = [[] for _ in range(8)]
    for l in range(depth):
        vec = lambda a: a[l][None, :]
        w1i, w1o = w_ffn1_in[l].astype(BF16), w_ffn1_out[l].astype(BF16)
        w2i, w2o = w_ffn2_in[l].astype(BF16), w_ffn2_out[l].astype(BF16)
        c_main = 2 * D_CONV + 3 * D_ATTN + D_IDX
        w_ki = w_in[l][:, c_main:c_main + IDX_DIM]
        w_wi = w_in[l][:, c_main + IDX_DIM:c_main + IDX_DIM + N_IDX_HEADS]
        w_proj = jnp.concatenate([w_in[l][:, :c_main], jnp.tile(w_ki, (1, N_IDX_HEADS)),
                                  jnp.pad(w_wi, ((0, 0), (0, LANES - N_IDX_HEADS)))], axis=1).astype(BF16)
        w_gates = w_in[l][:, c_main + IDX_DIM + N_IDX_HEADS:].astype(BF16)
        gq = jnp.tile(g_q[l], N_HEADS)[None, :]
        gk = jnp.tile(g_k[l], N_HEADS)[None, :]
        wc, wa, wo = w_cout[l].astype(BF16), w_aout[l].astype(BF16), w_o[l].astype(BF16)

        xp = _ffn(xp, vec(g_ffn1), w1i, w1o)
        xs = _ffn(xs, vec(g_ffn1), w1i, w1o)
        up, kp, vp, kip, qbp, kbp, vbp, qibp, ki8bp, wip = _proj(xp, vec(g_mix), w_proj, gq, gk, seg, tab_p, t, True)
        us, ks, vs, kis, qbs, kbs, vbs, qibs, ki8bs, wis = _proj(xs, vec(g_mix), w_proj, gq, gk, seg, tab_s, db * ts,
                                                                 False)

        up3, us3 = up.reshape(b, t, D_CONV), us.reshape(db, ts, D_CONV)
        lead = ((0, 0), (HALO - hist_rows, 0), (0, 0))
        hist_p = jnp.zeros((b, HALO, D_CONV), F32)
        hist_s = jnp.pad(state_conv[l], lead)
        conv_args = (w_dw[l], vec(b_dw), vec(g_cln), vec(b_cln))
        cyp = _conv(up3, hist_p, *conv_args)
        cys = _conv(us3, hist_s, *conv_args)
        st_p = jnp.concatenate([jnp.zeros((b, hist_rows, D_CONV), F32), up3], axis=1)[:, -hist_rows:]
        st_s = jnp.concatenate([state_conv[l], us3], axis=1)[:, -hist_rows:]

        r3 = lambda a, n, w: a.reshape(n, -1, w)
        att_p = _attn_prompt(r3(qbp, b, D_ATTN), r3(kbp, b, D_ATTN), r3(vbp, b, D_ATTN),
                             r3(qibp, b, D_IDX), r3(ki8bp, b, D_IDX), r3(wip, b, LANES))
        head_major = lambda a, nh, w: a.reshape(db, ts, nh, w).transpose(0, 2, 1, 3).reshape(db, nh * ts, w)
        new_page_t = lambda a: jnp.pad(a.reshape(db, ts, -1).transpose(0, 2, 1), ((0, 0), (0, 0), (0, PAGE_SIZE - ts)))
        att_s = _attn_sample(
            l, page_table, r3(qbs, db, D_ATTN), new_page_t(kbs), new_page_t(vbs),
            head_major(qibs, N_IDX_HEADS, IDX_DIM), new_page_t(ki8bs[:, :IDX_DIM]),
            head_major(wis[:, :N_IDX_HEADS], N_IDX_HEADS, 1),
            cache_k.transpose(0, 1, 3, 4, 2).reshape(depth, n_phys, D_ATTN, PAGE_SIZE),
            cache_v.transpose(0, 1, 3, 4, 2).reshape(depth, n_phys, D_ATTN, PAGE_SIZE),
            cache_idx_k.transpose(0, 1, 3, 2))

        xp = _merge(xp, cyp.reshape(b * t, D_CONV), att_p.reshape(b * t, D_ATTN), vec(g_mix), w_gates, wc, wa, wo)
        xs = _merge(xs, cys.reshape(db * ts, D_CONV), att_s.reshape(db * ts, D_ATTN), vec(g_mix), w_gates, wc, wa, wo)
        xp = _ffn(xp, vec(g_ffn2), w2i, w2o)
        xs = _ffn(xs, vec(g_ffn2), w2i, w2o)

        token_major = lambda a: a.reshape(b, N_HEADS, HEAD_DIM, t).transpose(0, 3, 1, 2)
        for lst, val in zip(outs, (token_major(kp), token_major(vp), kip.transpose(0, 2, 1), st_p,
                                   ks.reshape(db, ts, N_HEADS, HEAD_DIM), vs.reshape(db, ts, N_HEADS, HEAD_DIM),
                                   kis.reshape(db, ts, IDX_DIM), st_s)):
            lst.append(val)

    return (xp.reshape(b, t, d), xs.reshape(db, ts, d)) + tuple(jnp.stack(o) for o in outs)
```

```python
import functools
import math

import jax
import jax.numpy as jnp
from jax import lax
from jax.experimental import pallas as pl
from jax.experimental.pallas import tpu as pltpu

N_HEADS = 8
HEAD_DIM = 64
D_ATTN = N_HEADS * HEAD_DIM
D_CONV = 512
CONV_W = 31
N_IDX_HEADS = 8
IDX_DIM = 32
D_IDX = N_IDX_HEADS * IDX_DIM
TOPK_MAX = 256
PAGE_SIZE = 128
ROPE_THETA = 10000.0
EPS = 1e-6
IDX_W_SCALE = D_IDX ** -0.5

LANES = 128
SUBLANES = 8
MXU_TILE = 256
LOG2E = math.log2(math.e)
HALO = 32
INT_MIN = -2 ** 31
NEG_INF_KEY = INT_MIN + 0x7FFFFF
MASKED_LOGIT = -1e30
SOFTMAX_FLOOR = 0.25 * MASKED_LOGIT
COUNT_FOLD = 16 * LANES
VMEM_LIMIT = 56 * 1024 * 1024

F32 = jnp.float32
BF16 = jnp.bfloat16
I32 = jnp.int32
NT_DIMS = (((1,), (1,)), ((), ()))


def _row_tile(n, cap):
    t = cap
    while n % t:
        t //= 2
    return t


def _rms(x, g):
    return x * lax.rsqrt(jnp.mean(x * x, axis=-1, keepdims=True) + EPS) * g


def _const_spec(shape):
    nd = len(shape)
    return pl.BlockSpec(shape, lambda *_: (0,) * nd, pipeline_mode=pl.Buffered(1))


def _ffn_body(x_ref, g_ref, win_ref, wout_ref, o_ref, *, d_ff, n_chunks):
    x = x_ref[...]
    h = _rms(x, g_ref[...]).astype(BF16)
    fc = d_ff // n_chunks
    acc = jnp.zeros_like(x)
    for c in range(n_chunks):
        gate = jnp.dot(h, win_ref[:, c * fc:(c + 1) * fc], preferred_element_type=F32)
        up = jnp.dot(h, win_ref[:, d_ff + c * fc:d_ff + (c + 1) * fc], preferred_element_type=F32)
        act = (gate * jax.nn.sigmoid(gate) * up).astype(BF16)
        acc = acc + jnp.dot(act, wout_ref[c * fc:(c + 1) * fc, :], preferred_element_type=F32)
    o_ref[...] = x + 0.5 * acc


def _ffn(x, g, w_in_bf, w_out_bf):
    n, d = x.shape
    d_ff = w_out_bf.shape[0]
    tm = _row_tile(n, 512)
    n_chunks = 2 if d_ff % (2 * LANES) == 0 else 1
    return pl.pallas_call(
        functools.partial(_ffn_body, d_ff=d_ff, n_chunks=n_chunks),
        grid=(n // tm,),
        in_specs=[pl.BlockSpec((tm, d), lambda i: (i, 0)),
                  _const_spec((1, d)), _const_spec((d, 2 * d_ff)), _const_spec((d_ff, d))],
        out_specs=pl.BlockSpec((tm, d), lambda i: (i, 0)),
        out_shape=jax.ShapeDtypeStruct((n, d), F32),
        compiler_params=pltpu.CompilerParams(dimension_semantics=("parallel",), vmem_limit_bytes=VMEM_LIMIT),
        name="ffn",
    )(x, g, w_in_bf, w_out_bf)


def _rope(x, cos, sin_signed, half):
    n = x.shape[-1]
    lane = lax.broadcasted_iota(I32, x.shape, 1)
    first = (lane % (2 * half)) < half
    partner = jnp.where(first, pltpu.roll(x, n - half, 1), pltpu.roll(x, half, 1))
    return x * cos + partner * sin_signed


def _head_rms(x, seg_ref, g):
    x2 = x * x
    hi = x2.astype(BF16)
    lo = (x2 - hi.astype(F32)).astype(BF16)
    w = seg_ref.shape[0]
    ms = jnp.concatenate([jnp.dot(hi[:, c:c + w], seg_ref[...], preferred_element_type=F32)
                          + jnp.dot(lo[:, c:c + w], seg_ref[...], preferred_element_type=F32)
                          for c in range(0, x.shape[1], w)], axis=1)
    return x * lax.rsqrt(ms + EPS) * g


def _proj_body(x_ref, g_ref, w_ref, gq_ref, gk_ref, seg_ref, cqk_ref, sqk_ref, ci_ref, si_ref,
               u_ref, k_ref, v_ref, ki_ref, qb_ref, kb_ref, vb_ref, qib_ref, ki8b_ref, wi_ref, *, feature_major):
    h = _rms(x_ref[...], g_ref[...]).astype(BF16)
    p = jnp.dot(h, w_ref[...], preferred_element_type=F32)
    c0 = 0
    a_glu = p[:, c0:c0 + D_CONV]; c0 += D_CONV
    b_glu = p[:, c0:c0 + D_CONV]; c0 += D_CONV
    q = p[:, c0:c0 + D_ATTN]; c0 += D_ATTN
    k = p[:, c0:c0 + D_ATTN]; c0 += D_ATTN
    v = p[:, c0:c0 + D_ATTN]; c0 += D_ATTN
    qi = p[:, c0:c0 + D_IDX]; c0 += D_IDX
    ki8 = p[:, c0:c0 + D_IDX]; c0 += D_IDX
    wi = p[:, c0:c0 + LANES]

    u_ref[...] = a_glu * jax.nn.sigmoid(b_glu)
    q = _rope(_head_rms(q, seg_ref, gq_ref[...]), cqk_ref[...], sqk_ref[...], HEAD_DIM // 2)
    k = _rope(_head_rms(k, seg_ref, gk_ref[...]), cqk_ref[...], sqk_ref[...], HEAD_DIM // 2)
    qi = _rope(qi, ci_ref[...], si_ref[...], IDX_DIM // 2)
    ki8 = _rope(ki8, ci_ref[...], si_ref[...], IDX_DIM // 2)

    if feature_major:
        k_ref[0] = k.T
        v_ref[0] = v.T
        ki_ref[0] = ki8.T[:IDX_DIM, :]
    else:
        k_ref[...] = k
        v_ref[...] = v
        ki_ref[...] = ki8[:, :IDX_DIM]
    qb_ref[...] = (q * (HEAD_DIM ** -0.5 * LOG2E)).astype(BF16)
    kb_ref[...] = k.astype(BF16)
    vb_ref[...] = v.astype(BF16)
    qib_ref[...] = qi.astype(BF16)
    ki8b_ref[...] = ki8.astype(BF16)
    wi_ref[...] = wi * IDX_W_SCALE


def _proj(x, g, w_bf, gq, gk, seg, tables, tab_rows, feature_major):
    n, d = x.shape
    tm = _row_tile(math.gcd(n, tab_rows), 512)
    tab_blocks = tab_rows // tm
    row = lambda w: pl.BlockSpec((tm, w), lambda i: (i, 0))
    tab = lambda w: pl.BlockSpec((tm, w), lambda i: (i % tab_blocks, 0))
    f32o = lambda w: jax.ShapeDtypeStruct((n, w), F32)
    bfo = lambda w: jax.ShapeDtypeStruct((n, w), BF16)
    if feature_major:
        cache = lambda w: pl.BlockSpec((1, w, tm), lambda i: (i // tab_blocks, 0, i % tab_blocks))
        cacheo = lambda w: jax.ShapeDtypeStruct((n // tab_rows, w, tab_rows), F32)
    else:
        cache, cacheo = row, f32o
    return pl.pallas_call(
        functools.partial(_proj_body, feature_major=feature_major),
        grid=(n // tm,),
        in_specs=[row(d), _const_spec((1, d)), _const_spec(w_bf.shape),
                  _const_spec((1, D_ATTN)), _const_spec((1, D_ATTN)), _const_spec(seg.shape),
                  tab(D_ATTN), tab(D_ATTN), tab(D_IDX), tab(D_IDX)],
        out_specs=[row(D_CONV), cache(D_ATTN), cache(D_ATTN), cache(IDX_DIM),
                   row(D_ATTN), row(D_ATTN), row(D_ATTN), row(D_IDX), row(D_IDX), row(LANES)],
        out_shape=[f32o(D_CONV), cacheo(D_ATTN), cacheo(D_ATTN), cacheo(IDX_DIM),
                   bfo(D_ATTN), bfo(D_ATTN), bfo(D_ATTN), bfo(D_IDX), bfo(D_IDX), f32o(LANES)],
        compiler_params=pltpu.CompilerParams(dimension_semantics=("parallel",), vmem_limit_bytes=VMEM_LIMIT),
        name="proj",
    )(x, g, w_bf, gq, gk, seg, *tables)


def _conv_body(u_ref, uprev_ref, hist_ref, w_ref, b_ref, g_ref, bl_ref, o_ref, xs_ref, sh_ref, *, tc, n_t):
    xs_ref[HALO:HALO + tc, :] = u_ref[0]
    xs_ref[HALO + tc:HALO + tc + SUBLANES, :] = jnp.zeros((SUBLANES, D_CONV), F32)
    if n_t == 1:
        xs_ref[0:HALO, :] = hist_ref[0]
    else:
        i = pl.program_id(1)

        @pl.when(i == 0)
        def _():
            xs_ref[0:HALO, :] = hist_ref[0]

        @pl.when(i > 0)
        def _():
            xs_ref[0:HALO, :] = uprev_ref[0, tc - HALO:tc, :]

    n_sh = sh_ref.shape[1]
    for s in range(SUBLANES):
        sh_ref[s] = xs_ref[s:s + n_sh, :]

    rb = min(tc, 32)
    lead = HALO - (CONV_W - 1)
    for r0 in range(0, tc, rb):
        acc = jnp.broadcast_to(b_ref[...], (rb, D_CONV))
        for j in range(CONV_W):
            s = (lead + j) % SUBLANES
            base = r0 + lead + j - s
            acc = acc + w_ref[j:j + 1, :] * sh_ref[s, base:base + rb, :]
        mu = jnp.mean(acc, axis=-1, keepdims=True)
        yc = acc - mu
        y = yc * lax.rsqrt(jnp.mean(yc * yc, axis=-1, keepdims=True) + EPS) * g_ref[...] + bl_ref[...]
        o_ref[0, r0:r0 + rb, :] = y * jax.nn.sigmoid(y)


def _conv(u, hist, w_dw, b_dw, g_ln, b_ln):
    b, t, c = u.shape
    tc = _row_tile(t, 256)
    n_t = t // tc
    assert n_t == 1 or tc >= HALO
    vec = _const_spec((1, c))
    return pl.pallas_call(
        functools.partial(_conv_body, tc=tc, n_t=n_t),
        grid=(b, n_t),
        in_specs=[pl.BlockSpec((1, tc, c), lambda bi, i: (bi, i, 0)),
                  pl.BlockSpec((1, tc, c), lambda bi, i: (bi, jnp.maximum(i - 1, 0), 0)),
                  pl.BlockSpec((1, HALO, c), lambda bi, i: (bi, 0, 0)),
                  _const_spec((CONV_W, c)), vec, vec, vec],
        out_specs=pl.BlockSpec((1, tc, c), lambda bi, i: (bi, i, 0)),
        out_shape=jax.ShapeDtypeStruct((b, t, c), F32),
        scratch_shapes=[pltpu.VMEM((HALO + tc + SUBLANES, c), F32),
                        pltpu.VMEM((SUBLANES, HALO + tc, c), F32)],
        compiler_params=pltpu.CompilerParams(dimension_semantics=("parallel", "parallel")),
        name="conv",
    )(u, u, hist, w_dw, b_dw, g_ln, b_ln)


def _rank_to_f32(u):
    key = u + NEG_INF_KEY
    return lax.bitcast_convert_type(key ^ ((key >> 31) & 0x7FFFFFFF), F32)


def _count(mask):
    ones = jnp.where(mask, 1.0, 0.0)
    n = ones.shape[1]
    if n > COUNT_FOLD:
        parts = [ones[:, i:i + COUNT_FOLD] for i in range(0, n - n % COUNT_FOLD, COUNT_FOLD)]
        while len(parts) > 1:
            parts = [a + b for a, b in zip(parts[0::2], parts[1::2])] + parts[len(parts) & ~1:]
        total = jnp.sum(parts[0], axis=1, keepdims=True)
        if n % COUNT_FOLD:
            total = total + jnp.sum(ones[:, n - n % COUNT_FOLD:], axis=1, keepdims=True)
        return total
    return jnp.sum(ones, axis=1, keepdims=True)


def _select_topk_bias(sc_ref, code_ref, admissible, col, topk, n_idx_bits, bits_per_step):
    rows = sc_ref.shape[0]
    n_cand = 2 ** bits_per_step - 1
    assert 32 % bits_per_step == 0
    n_idx_bits = -(-n_idx_bits // bits_per_step) * bits_per_step

    def value_step(i, u):
        shift = 32 - bits_per_step * (i + 1)
        unit = jnp.int32(INT_MIN if shift == 31 else 1 << shift)
        sc = sc_ref[...]
        taken = jnp.zeros((rows, 1), I32)
        for j in range(1, n_cand + 1):
            taken = taken + jnp.where(_count(sc >= _rank_to_f32(u + j * unit)) >= topk, 1, 0)
        return u + taken * unit

    u = jnp.zeros((rows, 1), I32)
    for i in range(32 // bits_per_step):
        u = value_step(i, u)
    thr = _rank_to_f32(u)

    sc = sc_ref[...]
    code_ref[...] = jnp.where(sc > thr, -1, jnp.where(sc == thr, col, 2 ** n_idx_bits))

    def index_step(i, bound):
        unit = 1 << (n_idx_bits - bits_per_step * (i + 1))
        code = code_ref[...]
        taken = jnp.zeros((rows, 1), I32)
        for j in range(1, n_cand + 1):
            taken = taken + jnp.where(_count(code < bound + j * unit) < topk, 1, 0)
        return bound + taken * unit

    bound = jnp.zeros((rows, 1), I32)
    for i in range(n_idx_bits // bits_per_step):
        bound = index_step(i, bound)
    sc_ref[...] = jnp.where(admissible & (code_ref[...] <= bound), 0.0, MASKED_LOGIT)


def _pattern_to_f32(u, bf16_exact=False):
    key = jnp.maximum(u ^ INT_MIN, NEG_INF_KEY)
    bits = key ^ ((key >> 31) & 0x7FFFFFFF)
    if bf16_exact:
        bits = bits & -(1 << 16)
    return lax.bitcast_convert_type(bits, F32)


def _f32_to_pattern(x):
    bits = lax.bitcast_convert_type(x, I32)
    return (bits ^ ((bits >> 31) & 0x7FFFFFFF)) ^ INT_MIN


def _select_topk_bias_dense(sc_ref, sb_ref, tri_ref, topk):
    rows, n = sc_ref.shape

    def unrolled(n_steps, body, carry):
        for i in range(n_steps):
            carry = body(i, carry)
        return carry
    one, zero = jnp.ones((), BF16), jnp.zeros((), BF16)
    sb_ref[...] = sc_ref[...].astype(BF16)

    def count_bf16(cand):
        cb = jnp.broadcast_to(cand, (rows, LANES)).astype(BF16)
        acc = jnp.where(sb_ref[:, 0:LANES] >= cb, one, zero)
        for j in range(1, n // LANES):
            acc = acc + jnp.where(sb_ref[:, j * LANES:(j + 1) * LANES] >= cb, one, zero)
        return jnp.sum(acc.astype(F32), axis=1, keepdims=True)

    def coarse_step(i, p):
        cand = p + (1 << (15 - i))
        return jnp.where(count_bf16(_pattern_to_f32(cand << 16, bf16_exact=True)) >= topk, cand, p)

    p = unrolled(16, coarse_step, jnp.zeros((rows, 1), I32))
    base = _f32_to_pattern(_pattern_to_f32(p << 16, bf16_exact=True)) - (1 << 15)

    def fine_step(i, d):
        cand = d + (1 << (16 - i))
        return jnp.where(_count(sc_ref[...] >= _pattern_to_f32(base + cand)) >= topk, cand, d)

    thr = _pattern_to_f32(base + unrolled(17, fine_step, jnp.zeros((rows, 1), I32)))

    need = jnp.where(thr == -jnp.inf, 0.0, topk - _count(sc_ref[...] > thr))
    cw = tri_ref.shape[0]
    seen = jnp.zeros((rows, 1), F32)
    for c in range(n // cw):
        s = sc_ref[:, c * cw:(c + 1) * cw]
        tie = jnp.where(s == thr, 1.0, 0.0)
        earlier = jnp.dot(tie.astype(BF16), tri_ref[...], preferred_element_type=F32) + seen
        keep_tie = jnp.where(earlier <= need, 0.0, MASKED_LOGIT)
        sc_ref[:, c * cw:(c + 1) * cw] = jnp.where(s > thr, 0.0, jnp.where(s == thr, keep_tie, MASKED_LOGIT))
        seen = seen + jnp.sum(tie, axis=1, keepdims=True)


def _attn_prompt_body(q_ref, k_ref, v_ref, qi_ref, ki8_ref, wi_ref, tri_ref, o_ref, bias_ref, sb_ref,
                      *, tq, q0, topk):
    t = q0 + tq
    qi = qi_ref[0]
    ki8 = ki8_ref[0]
    wi = wi_ref[0]
    idx_head = lax.broadcasted_iota(I32, (1, D_IDX), 1) // IDX_DIM
    scores = jnp.zeros((tq, t), F32)
    for h in range(N_IDX_HEADS):
        qh = qi * jnp.where(idx_head == h, 1.0, 0.0).astype(BF16)
        dots = lax.dot_general(qh, ki8, NT_DIMS, preferred_element_type=F32)
        scores = scores + wi[:, h:h + 1] * jnp.maximum(dots, 0.0)

    col = lax.broadcasted_iota(I32, (tq, t), 1)
    qpos = q0 + lax.broadcasted_iota(I32, (tq, t), 0)
    bias_ref[...] = jnp.where(col <= qpos, scores, -jnp.inf)
    _select_topk_bias_dense(bias_ref, sb_ref, tri_ref, topk)

    pair_head = lax.broadcasted_iota(I32, (1, LANES), 1) // HEAD_DIM
    for j in range(D_ATTN // LANES):
        sl = slice(j * LANES, (j + 1) * LANES)
        qj = q_ref[0, :, sl]
        kj = k_ref[0, :, sl]
        vj = jnp.concatenate([v_ref[0, :, sl], jnp.ones((t, LANES), BF16)], axis=1)
        out = jnp.zeros((tq, LANES), F32)
        for half in range(LANES // HEAD_DIM):
            mine = pair_head == half
            qh = qj * jnp.where(mine, 1.0, 0.0).astype(BF16)
            logits = lax.dot_general(qh, kj, NT_DIMS, preferred_element_type=F32) + bias_ref[...]
            m = jnp.max(logits, axis=1, keepdims=True)
            p = jnp.exp2(logits - m)
            pv = jnp.dot(p.astype(BF16), vj, preferred_element_type=F32)
            out = jnp.where(mine, pv[:, :LANES] / pv[:, LANES:], out)
        o_ref[0, :, sl] = out


def _attn_prompt(q_bf, k_bf, v_bf, qi_bf, ki8_bf, wi):
    b, t, _ = q_bf.shape
    tq = _row_tile(t, 256)
    topk = min(TOPK_MAX, t // 4)
    cw = min(tq, 2 * LANES)
    assert tq % cw == 0 and cw % LANES == 0
    tri = jnp.triu(jnp.ones((cw, cw), BF16))
    blocks = []
    for qb in range(t // tq):
        kv = (qb + 1) * tq
        qspec = lambda w, qb=qb: pl.BlockSpec((1, tq, w), lambda bi: (bi, qb, 0))
        kspec = lambda w, kv=kv: pl.BlockSpec((1, kv, w), lambda bi: (bi, 0, 0))
        blocks.append(pl.pallas_call(
            functools.partial(_attn_prompt_body, tq=tq, q0=qb * tq, topk=topk),
            grid=(b,),
            in_specs=[qspec(D_ATTN), kspec(D_ATTN), kspec(D_ATTN), qspec(D_IDX), kspec(D_IDX), qspec(LANES),
                      pl.BlockSpec((cw, cw), lambda bi: (0, 0))],
            out_specs=pl.BlockSpec((1, tq, D_ATTN), lambda bi: (bi, 0, 0)),
            out_shape=jax.ShapeDtypeStruct((b, tq, D_ATTN), F32),
            scratch_shapes=[pltpu.VMEM((tq, kv), F32), pltpu.VMEM((tq, kv), BF16)],
            compiler_params=pltpu.CompilerParams(dimension_semantics=("parallel",), vmem_limit_bytes=VMEM_LIMIT),
            name=f"attn_prompt_q{qb}",
        )(q_bf, k_bf, v_bf, qi_bf, ki8_bf, wi, tri))
    return jnp.concatenate(blocks, axis=1)


def _attn_sample_body(pt_ref, q_ref, knt_ref, vnt_ref, qi_ref, kint_ref, wrow_ref, ck_hbm, cv_hbm, cik_hbm,
                      o_ref, ikbuf, kbuf, vbuf, bias_ref, code_ref, sem_ik, sem_k, sem_v,
                      *, layer, n_seq, n_pages, ts, pc_ik, pc_kv, kv_slots, topk):
    seq = pl.program_id(0)
    n_past = n_pages * PAGE_SIZE
    s_pad = n_past + PAGE_SIZE
    rows = N_HEADS * ts
    n_ik_chunks = n_pages // pc_ik
    n_kv_chunks = n_pages // pc_kv
    n_kv_total = n_seq * n_kv_chunks

    def page_copy(hbm, buf, sem, s, page, slot, i):
        return pltpu.make_async_copy(hbm.at[layer, pt_ref[s, page]], buf.at[slot, i], sem.at[slot])

    def ik_copies(s):
        return [page_copy(cik_hbm, ikbuf, sem_ik, s, p, s % 2, p) for p in range(n_pages)]

    def kv_copies(g):
        s, c = g // n_kv_chunks, g % n_kv_chunks
        slot = g % kv_slots
        return [page_copy(hbm, buf, sem, s, c * pc_kv + i, slot, i)
                for hbm, buf, sem in ((ck_hbm, kbuf, sem_k), (cv_hbm, vbuf, sem_v)) for i in range(pc_kv)]

    def start_ik(s):
        for n, cp in enumerate(ik_copies(s)):
            cp.start(priority=n % 2)

    def start_kv(g):
        for n, cp in enumerate(kv_copies(g)):
            cp.start(priority=n // pc_kv)

    @pl.when(seq == 0)
    def _():
        start_ik(0)
        for g in range(min(kv_slots - 1, n_kv_total)):
            start_kv(g)

    @pl.when(seq + 1 < n_seq)
    def _():
        start_ik(seq + 1)

    qi = qi_ref[0]
    wrow = wrow_ref[0]

    def page_scores(keys_t_bf):
        r = jnp.maximum(jnp.dot(qi, keys_t_bf, preferred_element_type=F32), 0.0) * wrow
        s = r[0:ts]
        for h in range(1, N_IDX_HEADS):
            s = s + r[h * ts:(h + 1) * ts]
        return s

    for cp in ik_copies(seq):
        cp.wait()

    def ik_chunk(c, carry):
        for i in range(pc_ik):
            page = c * pc_ik + i
            off = pl.multiple_of(page * PAGE_SIZE, PAGE_SIZE)
            bias_ref[:, pl.ds(off, PAGE_SIZE)] = page_scores(ikbuf[seq % 2, page].astype(BF16))
        return carry

    lax.fori_loop(0, n_ik_chunks, ik_chunk, 0)
    bias_ref[:, n_past:s_pad] = page_scores(kint_ref[0])

    col = lax.broadcasted_iota(I32, (ts, s_pad), 1)
    tpos = lax.broadcasted_iota(I32, (ts, s_pad), 0)
    admissible = (col - n_past) <= tpos
    bias_ref[...] = jnp.where(admissible, bias_ref[...], -jnp.inf)
    _select_topk_bias(bias_ref, code_ref, admissible, col, topk, max(1, (s_pad - 1).bit_length()), bits_per_step=2)

    qf = q_ref[0].astype(F32)
    row_head = lax.broadcasted_iota(I32, (rows, D_ATTN), 0) // ts
    lane_head = lax.broadcasted_iota(I32, (rows, D_ATTN), 1) // HEAD_DIM
    qbd = jnp.where(row_head == lane_head, jnp.concatenate([qf] * N_HEADS, axis=0), 0.0).astype(BF16)

    def attend(carry, pages_k, pages_v, bias):
        m, l, acc = carry
        n = len(pages_k) * PAGE_SIZE
        logits = jnp.concatenate([jnp.dot(qbd, kt, preferred_element_type=F32) for kt in pages_k], axis=1)
        logits = (logits.reshape(N_HEADS, ts, n) + bias[None]).reshape(rows, n)
        m_new = jnp.maximum(m, jnp.max(logits, axis=1, keepdims=True))
        alpha = jnp.exp2(m - m_new)
        p = jnp.exp2(logits - m_new)
        l = alpha * l + jnp.sum(p, axis=1, keepdims=True)
        p = p.astype(BF16)
        acc = alpha * acc
        for i, vt in enumerate(pages_v):
            acc = acc + lax.dot_general(p[:, i * PAGE_SIZE:(i + 1) * PAGE_SIZE], vt, NT_DIMS,
                                        preferred_element_type=F32)
        return m_new, l, acc

    n_chunk_keys = pc_kv * PAGE_SIZE

    def kv_chunk(c, carry):
        g = seq * n_kv_chunks + c
        slot = g % kv_slots

        @pl.when(g + kv_slots - 1 < n_kv_total)
        def _():
            start_kv(g + kv_slots - 1)

        for cp in kv_copies(g):
            cp.wait()
        off = pl.multiple_of(c * n_chunk_keys, n_chunk_keys)
        return attend(carry, [kbuf[slot, i].astype(BF16) for i in range(pc_kv)],
                      [vbuf[slot, i].astype(BF16) for i in range(pc_kv)], bias_ref[:, pl.ds(off, n_chunk_keys)])

    init = (jnp.full((rows, 1), SOFTMAX_FLOOR, F32), jnp.zeros((rows, 1), F32), jnp.zeros((rows, D_ATTN), F32))
    carry = lax.fori_loop(0, n_kv_chunks, kv_chunk, init)
    _, l, acc = attend(carry, [knt_ref[0]], [vnt_ref[0]], bias_ref[:, n_past:s_pad])

    acc = acc / l
    out = jnp.zeros((ts, D_ATTN), F32)
    head_of_lane = lax.broadcasted_iota(I32, (ts, D_ATTN), 1) // HEAD_DIM
    for h in range(N_HEADS):
        out = jnp.where(head_of_lane == h, acc[h * ts:(h + 1) * ts], out)
    o_ref[0] = out


def _attn_sample(layer, page_table, q_bf, knt_bf, vnt_bf, qi_rows, kint_bf, wrow, cache_kt, cache_vt, cache_ikt):
    db, ts, _ = q_bf.shape
    n_pages = page_table.shape[1]
    assert ts == SUBLANES, "one query block of 8 sublanes per sequence"
    pc_ik = _row_tile(n_pages, 32)
    pc_kv = _row_tile(n_pages, 16)
    kv_slots = 5
    s_pad = (n_pages + 1) * PAGE_SIZE
    topk = min(TOPK_MAX, (n_pages * PAGE_SIZE + ts) // 4)
    rows = N_HEADS * ts
    per_seq = lambda r, w: pl.BlockSpec((1, r, w), lambda s, pt: (s, 0, 0))
    any_spec = pl.BlockSpec(memory_space=pl.ANY)
    grid_spec = pltpu.PrefetchScalarGridSpec(
        num_scalar_prefetch=1,
        grid=(db,),
        in_specs=[per_seq(ts, D_ATTN), per_seq(D_ATTN, PAGE_SIZE), per_seq(D_ATTN, PAGE_SIZE),
                  per_seq(rows, IDX_DIM), per_seq(IDX_DIM, PAGE_SIZE), per_seq(rows, 1),
                  any_spec, any_spec, any_spec],
        out_specs=per_seq(ts, D_ATTN),
        scratch_shapes=[pltpu.VMEM((2, n_pages, IDX_DIM, PAGE_SIZE), F32),
                        pltpu.VMEM((kv_slots, pc_kv, D_ATTN, PAGE_SIZE), F32),
                        pltpu.VMEM((kv_slots, pc_kv, D_ATTN, PAGE_SIZE), F32),
                        pltpu.VMEM((ts, s_pad), F32), pltpu.VMEM((ts, s_pad), I32),
                        pltpu.SemaphoreType.DMA((2,)), pltpu.SemaphoreType.DMA((kv_slots,)),
                        pltpu.SemaphoreType.DMA((kv_slots,))])
    return pl.pallas_call(
        functools.partial(_attn_sample_body, layer=layer, n_seq=db, n_pages=n_pages, ts=ts, pc_ik=pc_ik,
                          pc_kv=pc_kv, kv_slots=kv_slots, topk=topk),
        grid_spec=grid_spec,
        out_shape=jax.ShapeDtypeStruct((db, ts, D_ATTN), F32),
        compiler_params=pltpu.CompilerParams(dimension_semantics=("arbitrary",), vmem_limit_bytes=VMEM_LIMIT),
        name="attn_sample",
    )(page_table, q_bf, knt_bf, vnt_bf, qi_rows, kint_bf, wrow, cache_kt, cache_vt, cache_ikt)


def _merge_body(x_ref, cy_ref, att_ref, g_ref, wg_ref, wc_ref, wa_ref, wo_ref, o_ref, *, d):
    x = x_ref[...]
    h = _rms(x, g_ref[...]).astype(BF16)
    gates = jax.nn.sigmoid(jnp.dot(h, wg_ref[...], preferred_element_type=F32))
    conv_out = jnp.dot(cy_ref[...].astype(BF16), wc_ref[...], preferred_element_type=F32)
    a = jnp.dot(att_ref[...].astype(BF16), wa_ref[...], preferred_element_type=F32)
    mixed = (gates[:, :d] * conv_out + gates[:, d:] * a).astype(BF16)
    o_ref[...] = x + jnp.dot(mixed, wo_ref[...], preferred_element_type=F32)


def _merge(x, conv_y, att, g, wg_bf, wc_bf, wa_bf, wo_bf):
    n, d = x.shape
    tm = _row_tile(n, 512)
    row = lambda w: pl.BlockSpec((tm, w), lambda i: (i, 0))
    return pl.pallas_call(
        functools.partial(_merge_body, d=d),
        grid=(n // tm,),
        in_specs=[row(d), row(D_CONV), row(D_ATTN), _const_spec((1, d)), _const_spec(wg_bf.shape),
                  _const_spec(wc_bf.shape), _const_spec(wa_bf.shape), _const_spec(wo_bf.shape)],
        out_specs=row(d),
        out_shape=jax.ShapeDtypeStruct((n, d), F32),
        compiler_params=pltpu.CompilerParams(dimension_semantics=("parallel",), vmem_limit_bytes=VMEM_LIMIT),
        name="merge",
    )(x, conv_y, att, g, wg_bf, wc_bf, wa_bf, wo_bf)


def _rope_tables(pos, head_dim, n_heads):
    half = head_dim // 2
    inv = jnp.float32(ROPE_THETA) ** (-jnp.arange(half, dtype=F32) * (2.0 / head_dim))
    ang = pos.astype(F32)[:, None] * inv[None, :]
    cos = jnp.tile(jnp.concatenate([jnp.cos(ang), jnp.cos(ang)], axis=1), (1, n_heads))
    sin = jnp.tile(jnp.concatenate([-jnp.sin(ang), jnp.sin(ang)], axis=1), (1, n_heads))
    return cos, sin


def kernel(x_prompt, x_sample, cache_k, cache_v, cache_idx_k, state_conv, page_table, g_ffn1, w_ffn1_in, w_ffn1_out, g_mix, w_in, g_q, g_k, w_dw, b_dw, g_cln, b_cln, w_cout, w_aout, w_o, g_ffn2, w_ffn2_in, w_ffn2_out):
    b, t, d = x_prompt.shape
    db, ts, _ = x_sample.shape
    depth = g_mix.shape[0]
    n_pages = page_table.shape[1]
    n_phys = cache_k.shape[1]
    past_len = n_pages * PAGE_SIZE
    hist_rows = CONV_W - 1

    pos_p = jnp.arange(t)
    pos_s = jnp.tile(past_len + jnp.arange(ts), db)
    tab_p = _rope_tables(pos_p, HEAD_DIM, N_HEADS) + _rope_tables(pos_p, IDX_DIM, N_IDX_HEADS)
    tab_s = _rope_tables(pos_s, HEAD_DIM, N_HEADS) + _rope_tables(pos_s, IDX_DIM, N_IDX_HEADS)
    seg = jnp.kron(jnp.eye(MXU_TILE // HEAD_DIM, dtype=F32),
                   jnp.full((HEAD_DIM, HEAD_DIM), 1.0 / HEAD_DIM, F32)).astype(BF16)

    xp = x_prompt.reshape(b * t, d)
    xs = x_sample.reshape(db * ts, d)
    outs = [[] for _ in range(8)]
    for l in range(depth):
        vec = lambda a: a[l][None, :]
        w1i, w1o = w_ffn1_in[l].astype(BF16), w_ffn1_out[l].astype(BF16)
        w2i, w2o = w_ffn2_in[l].astype(BF16), w_ffn2_out[l].astype(BF16)
        c_main = 2 * D_CONV + 3 * D_ATTN + D_IDX
        w_ki = w_in[l][:, c_main:c_main + IDX_DIM]
        w_wi = w_in[l][:, c_main + IDX_DIM:c_main + IDX_DIM + N_IDX_HEADS]
        w_proj = jnp.concatenate([w_in[l][:, :c_main], jnp.tile(w_ki, (1, N_IDX_HEADS)),
                                  jnp.pad(w_wi, ((0, 0), (0, LANES - N_IDX_HEADS)))], axis=1).astype(BF16)
        w_gates = w_in[l][:, c_main + IDX_DIM + N_IDX_HEADS:].astype(BF16)
        gq = jnp.tile(g_q[l], N_HEADS)[None, :]
        gk = jnp.tile(g_k[l], N_HEADS)[None, :]
        wc, wa, wo = w_cout[l].astype(BF16), w_aout[l].astype(BF16), w_o[l].astype(BF16)

        xp = _ffn(xp, vec(g_ffn1), w1i, w1o)
        xs = _ffn(xs, vec(g_ffn1), w1i, w1o)
        up, kp, vp, kip, qbp, kbp, vbp, qibp, ki8bp, wip = _proj(xp, vec(g_mix), w_proj, gq, gk, seg, tab_p, t, True)
        us, ks, vs, kis, qbs, kbs, vbs, qibs, ki8bs, wis = _proj(xs, vec(g_mix), w_proj, gq, gk, seg, tab_s, db * ts,
                                                                 False)

        up3, us3 = up.reshape(b, t, D_CONV), us.reshape(db, ts, D_CONV)
        lead = ((0, 0), (HALO - hist_rows, 0), (0, 0))
        hist_p = jnp.zeros((b, HALO, D_CONV), F32)
        hist_s = jnp.pad(state_conv[l], lead)
        conv_args = (w_dw[l], vec(b_dw), vec(g_cln), vec(b_cln))
        cyp = _conv(up3, hist_p, *conv_args)
        cys = _conv(us3, hist_s, *conv_args)
        st_p = jnp.concatenate([jnp.zeros((b, hist_rows, D_CONV), F32), up3], axis=1)[:, -hist_rows:]
        st_s = jnp.concatenate([state_conv[l], us3], axis=1)[:, -hist_rows:]

        r3 = lambda a, n, w: a.reshape(n, -1, w)
        att_p = _attn_prompt(r3(qbp, b, D_ATTN), r3(kbp, b, D_ATTN), r3(vbp, b, D_ATTN),
                             r3(qibp, b, D_IDX), r3(ki8bp, b, D_IDX), r3(wip, b, LANES))
        head_major = lambda a, nh, w: a.reshape(db, ts, nh, w).transpose(0, 2, 1, 3).reshape(db, nh * ts, w)
        new_page_t = lambda a: jnp.pad(a.reshape(db, ts, -1).transpose(0, 2, 1), ((0, 0), (0, 0), (0, PAGE_SIZE - ts)))
        att_s = _attn_sample(
            l, page_table, r3(qbs, db, D_ATTN), new_page_t(kbs), new_page_t(vbs),
            head_major(qibs, N_IDX_HEADS, IDX_DIM), new_page_t(ki8bs[:, :IDX_DIM]),
            head_major(wis[:, :N_IDX_HEADS], N_IDX_HEADS, 1),
            cache_k.transpose(0, 1, 3, 4, 2).reshape(depth, n_phys, D_ATTN, PAGE_SIZE),
            cache_v.transpose(0, 1, 3, 4, 2).reshape(depth, n_phys, D_ATTN, PAGE_SIZE),
            cache_idx_k.transpose(0, 1, 3, 2))

        xp = _merge(xp, cyp.reshape(b * t, D_CONV), att_p.reshape(b * t, D_ATTN), vec(g_mix), w_gates, wc, wa, wo)
        xs = _merge(xs, cys.reshape(db * ts, D_CONV), att_s.reshape(db * ts, D_ATTN), vec(g_mix), w_gates, wc, wa, wo)
        xp = _ffn(xp, vec(g_ffn2), w2i, w2o)
        xs = _ffn(xs, vec(g_ffn2), w2i, w2o)

        token_major = lambda a: a.reshape(b, N_HEADS, HEAD_DIM, t).transpose(0, 3, 1, 2)
        for lst, val in zip(outs, (token_major(kp), token_major(vp), kip.transpose(0, 2, 1), st_p,
                                   ks.reshape(db, ts, N_HEADS, HEAD_DIM), vs.reshape(db, ts, N_HEADS, HEAD_DIM),
                                   kis.reshape(db, ts, IDX_DIM), st_s)):
            lst.append(val)

    return (xp.reshape(b, t, d), xs.reshape(db, ts, d)) + tuple(jnp.stack(o) for o in outs)
```
